```python
import math
import jax, jax.numpy as jnp
from jax import lax
import numpy as np

D_MODEL = 1024
BATCH = 1
SEQ = 16384
DEPTH = 1
DEC_BATCH = 32
DEC_SEQ = 16
PAST_LEN = 4096

CHUNK = 64
D_MIX = 2 * D_MODEL
D_SSM = D_MIX // 2
SSD_HEAD_DIM = 64
N_SSD_HEADS = D_SSM // SSD_HEAD_DIM
N_SSD_GROUPS = 2
D_STATE = 128
CONV_W = 4
CONV_DIM = D_SSM + 2 * N_SSD_GROUPS * D_STATE
D_ATT = D_MIX - D_SSM
ATT_HEAD_DIM = 128
N_ATT_HEADS = D_ATT // ATT_HEAD_DIM
N_KV_HEADS = 2
IDX_HEADS = 8
IDX_DIM = 64
TOPK_MAX = 256
Q_BLOCK = 128
ALPHA = (2.0 * DEPTH) ** 0.25
BETA = (8.0 * DEPTH) ** -0.25
LN_EPS = 1e-5
RMS_EPS = 1e-5
SPLITS = (D_SSM,
          CONV_DIM,
          N_SSD_HEADS,
          N_ATT_HEADS * ATT_HEAD_DIM,
          N_KV_HEADS * ATT_HEAD_DIM,
          N_KV_HEADS * ATT_HEAD_DIM,
          D_ATT,
          IDX_HEADS * IDX_DIM,
          IDX_DIM,
          IDX_HEADS)
D_IN = sum(SPLITS)

kernel_name = 'hybrid_ssd_dsa_streaming_step'

F32 = jnp.float32


def project(x, w_in):
    B, L, _ = x.shape
    h = jnp.einsum('bld,de->ble', x, w_in)
    cuts = np.cumsum(SPLITS)[:-1].tolist()
    z_s, xbc, dtr, q, k, v, z_a, qi, ki, wi = jnp.split(h, cuts, axis=-1)
    q = q.reshape(B, L, N_KV_HEADS, N_ATT_HEADS // N_KV_HEADS, ATT_HEAD_DIM)
    k = k.reshape(B, L, N_KV_HEADS, ATT_HEAD_DIM)
    v = v.reshape(B, L, N_KV_HEADS, ATT_HEAD_DIM)
    qi = qi.reshape(B, L, IDX_HEADS, IDX_DIM)
    return z_s, xbc, dtr, q, k, v, z_a, qi, ki, wi


def alibi_slopes():
    return 2.0 ** (-8.0 * jnp.arange(1, N_ATT_HEADS + 1, dtype=F32) / N_ATT_HEADS)


def causal_dwconv(xbc, conv_prev, conv_w, conv_b):
    L = xbc.shape[1]
    xpad = jnp.concatenate([conv_prev.astype(xbc.dtype), xbc], axis=1)
    out = conv_b + xpad[:, 0:L] * conv_w[0]
    for j in range(1, CONV_W):
        out = out + xpad[:, j:j + L] * conv_w[j]
    return jax.nn.silu(out), xpad[:, -(CONV_W - 1):]


def ssd_scan(x, dt, a_neg, bm, cm, s0, q_len):
    B, L, H, P = x.shape
    G = N_SSD_GROUPS
    R = H // G
    N = bm.shape[-1]
    C = L // q_len
    x = x.reshape(B, C, q_len, G, R, P)
    dt = dt.reshape(B, C, q_len, G, R)
    bm = bm.reshape(B, C, q_len, G, N)
    cm = cm.reshape(B, C, q_len, G, N)
    a = dt * a_neg.reshape(G, R)
    xd = x * dt[..., None]
    acum = jnp.cumsum(a, axis=2)
    causal = jnp.tril(jnp.ones((q_len, q_len), dtype=bool))
    seg = acum[:, :, :, None] - acum[:, :, None, :]
    decay = jnp.exp(jnp.where(causal[:, :, None, None], seg, -jnp.inf))
    cb = jnp.einsum('bcign,bcjgn->bcijg', cm, bm)
    y_diag = jnp.einsum('bcijgr,bcjgrp->bcigrp', cb[..., None] * decay, xd)
    to_end = jnp.exp(acum[:, :, -1:] - acum)
    chunk_states = jnp.einsum('bcjgn,bcjgrp->bcgrpn', bm, xd * to_end[..., None])
    chunk_decay = jnp.exp(acum[:, :, -1])

    def step(s, inp):
        st, dec = inp
        return dec[..., None, None] * s + st, s

    s_final, s_prev = lax.scan(step, s0.reshape(B, G, R, P, N),
                               (jnp.moveaxis(chunk_states, 1, 0), jnp.moveaxis(chunk_decay, 1, 0)))
    s_prev = jnp.moveaxis(s_prev, 0, 1)
    y_off = jnp.einsum('bcign,bcgrpn->bcigrp', cm, s_prev) * jnp.exp(acum)[..., None]
    y = (y_diag + y_off).reshape(B, L, H, P)
    return y, s_final.reshape(B, H, P, N)


def ssd_branch(z, xbc, dt_raw, conv_prev, ssm_prev, q_len,
               conv_w, conv_b, dt_bias, a_log, d_skip, norm_w):
    B, L, _ = z.shape
    xbc, conv_new = causal_dwconv(xbc, conv_prev, conv_w, conv_b)
    xs, bm, cm = jnp.split(xbc, [D_SSM, D_SSM + N_SSD_GROUPS * D_STATE], axis=-1)
    xs = xs.reshape(B, L, N_SSD_HEADS, SSD_HEAD_DIM).astype(F32)
    bm = bm.reshape(B, L, N_SSD_GROUPS, D_STATE).astype(F32)
    cm = cm.reshape(B, L, N_SSD_GROUPS, D_STATE).astype(F32)
    dt = jax.nn.softplus(dt_raw.astype(F32) + dt_bias.astype(F32))
    a_neg = -jnp.exp(a_log.astype(F32))
    y, ssm_new = ssd_scan(xs, dt, a_neg, bm, cm, ssm_prev.astype(F32), q_len)
    y = y + d_skip.astype(F32)[:, None] * xs
    g = y.reshape(B, L, D_SSM) * jax.nn.silu(z.astype(F32))
    g = g.reshape(B, L, N_SSD_GROUPS, D_SSM // N_SSD_GROUPS)
    g = g * lax.rsqrt(jnp.mean(g * g, axis=-1, keepdims=True) + RMS_EPS)
    out = g.reshape(B, L, D_SSM) * norm_w.astype(F32)
    return out.astype(z.dtype), conv_new, ssm_new.astype(ssm_prev.dtype)


def dsa_attend(q, qi, wi, qpos, k_all, v_all, ki_all, k_sel):
    L = k_all.shape[1]
    kpos = jnp.arange(L, dtype=jnp.int32)
    s = jnp.einsum('bqhd,bld->bqhl', qi.astype(F32), ki_all.astype(F32)) * IDX_DIM ** -0.5
    score = jnp.einsum('bqh,bqhl->bql', wi.astype(F32) * IDX_HEADS ** -0.5, jax.nn.relu(s))
    chunk_end = (qpos // CHUNK + 1) * CHUNK - 1
    admissible = kpos[None, :] <= chunk_end[:, None]
    score = jnp.where(admissible[None], score, -jnp.inf)
    top_val, top_idx = lax.top_k(score, k_sel)
    valid = jnp.isfinite(top_val)
    gather = jax.vmap(lambda rows, ids: rows[ids])
    k_g = gather(k_all, top_idx).astype(F32)
    v_g = gather(v_all, top_idx).astype(F32)
    logits = jnp.einsum('bqgrd,bqkgd->bqgrk', q.astype(F32), k_g) * ATT_HEAD_DIM ** -0.5
    dist = jnp.abs(qpos[None, :, None] - top_idx).astype(F32)
    slopes = alibi_slopes().reshape(N_KV_HEADS, N_ATT_HEADS // N_KV_HEADS)
    logits = logits - slopes[None, None, :, :, None] * dist[:, :, None, None, :]
    logits = jnp.where(valid[:, :, None, None, :], logits, -jnp.inf)
    p = jax.nn.softmax(logits, axis=-1)
    out = jnp.einsum('bqgrk,bqkgd->bqgrd', p, v_g)
    return out.astype(q.dtype)


def merge_and_norm(x, y_ssd, y_att, z_att, w_out, ln_g, ln_b):
    B, L, _ = x.shape
    att = y_att.reshape(B, L, D_ATT) * jax.nn.silu(z_att)
    mix = jnp.einsum('ble,ed->bld', jnp.concatenate([y_ssd, att], axis=-1), w_out)
    h = ALPHA * x.astype(F32) + mix.astype(F32)
    mu = jnp.mean(h, axis=-1, keepdims=True)
    hc = h - mu
    var = jnp.mean(hc * hc, axis=-1, keepdims=True)
    y = hc * lax.rsqrt(var + LN_EPS) * ln_g.astype(F32) + ln_b.astype(F32)
    return y.astype(x.dtype)


def setup_inputs(seed: int = 0) -> dict:
    key = jax.random.key(seed)
    ks = jax.random.split(key, 18)
    nrm = jax.random.normal
    dt0 = jnp.exp(jax.random.uniform(ks[9], (N_SSD_HEADS,), minval=math.log(1e-3), maxval=math.log(1e-1)))
    return {
        'x_prompt': nrm(ks[0], (BATCH, SEQ, D_MODEL), F32),
        'x_sample': nrm(ks[1], (DEC_BATCH, DEC_SEQ, D_MODEL), F32),
        'cache_k': nrm(ks[2], (DEC_BATCH, PAST_LEN, N_KV_HEADS, ATT_HEAD_DIM), F32),
        'cache_v': nrm(ks[3], (DEC_BATCH, PAST_LEN, N_KV_HEADS, ATT_HEAD_DIM), F32),
        'cache_kidx': nrm(ks[4], (DEC_BATCH, PAST_LEN, IDX_DIM), F32),
        'state_ssm': 0.1 * nrm(ks[5], (DEC_BATCH, N_SSD_HEADS, SSD_HEAD_DIM, D_STATE), F32),
        'state_conv': nrm(ks[6], (DEC_BATCH, CONV_W - 1, CONV_DIM), F32),
        'w_in': nrm(ks[7], (D_MODEL, D_IN), F32) * D_MODEL ** -0.5,
        'conv_w': nrm(ks[8], (CONV_W, CONV_DIM), F32) * CONV_W ** -0.5,
        'conv_b': 0.01 * nrm(ks[10], (CONV_DIM,), F32),
        'dt_bias': dt0 + jnp.log(-jnp.expm1(-dt0)),
        'a_log': jnp.log(jax.random.uniform(ks[11], (N_SSD_HEADS,), minval=1.0, maxval=16.0)),
        'd_skip': 1.0 + 0.1 * nrm(ks[12], (N_SSD_HEADS,), F32),
        'ssd_norm_w': 1.0 + 0.1 * nrm(ks[13], (D_SSM,), F32),
        'w_out': nrm(ks[14], (D_MIX, D_MODEL), F32) * (D_MIX ** -0.5) * BETA,
        'ln_g': 1.0 + 0.1 * nrm(ks[15], (D_MODEL,), F32),
        'ln_b': 0.01 * nrm(ks[16], (D_MODEL,), F32),
    }


def reference(x_prompt, x_sample, cache_k, cache_v, cache_kidx, state_ssm, state_conv,
              w_in, conv_w, conv_b, dt_bias, a_log, d_skip, ssd_norm_w, w_out, ln_g, ln_b):
    for _layer in range(DEPTH):
        B, S, _ = x_prompt.shape
        z_s, xbc, dtr, q, k_p, v_p, z_a, qi, ki_p, wi = project(x_prompt, w_in)
        conv0 = jnp.zeros((B, CONV_W - 1, CONV_DIM), x_prompt.dtype)
        ssm0 = jnp.zeros((B, N_SSD_HEADS, SSD_HEAD_DIM, D_STATE), state_ssm.dtype)
        y_ssd_p, conv_p, ssm_p = ssd_branch(z_s, xbc, dtr, conv0, ssm0, CHUNK,
                                            conv_w, conv_b, dt_bias, a_log, d_skip, ssd_norm_w)
        k_sel_p = min(TOPK_MAX, S // 4)
        nb = S // Q_BLOCK

        def blocks(a):
            return jnp.moveaxis(a.reshape((B, nb, Q_BLOCK) + a.shape[2:]), 1, 0)

        qpos_p = jnp.arange(S, dtype=jnp.int32).reshape(nb, Q_BLOCK)
        att_p = lax.map(lambda a: dsa_attend(a[0], a[1], a[2], a[3], k_p, v_p, ki_p, k_sel_p),
                        (blocks(q), blocks(qi), blocks(wi), qpos_p))
        att_p = jnp.moveaxis(att_p, 0, 1).reshape(B, S, D_ATT)
        y_prompt = merge_and_norm(x_prompt, y_ssd_p, att_p, z_a, w_out, ln_g, ln_b)

        Bd, T, _ = x_sample.shape
        P = cache_k.shape[1]
        z_s2, xbc2, dtr2, q2, k_s, v_s, z_a2, qi2, ki_s, wi2 = project(x_sample, w_in)
        y_ssd_s, conv_s, ssm_s = ssd_branch(z_s2, xbc2, dtr2, state_conv, state_ssm, T,
                                            conv_w, conv_b, dt_bias, a_log, d_skip, ssd_norm_w)
        k_all = jnp.concatenate([cache_k, k_s], axis=1)
        v_all = jnp.concatenate([cache_v, v_s], axis=1)
        ki_all = jnp.concatenate([cache_kidx, ki_s], axis=1)
        k_sel_s = min(TOPK_MAX, (P + T) // 4)
        qpos_s = P + jnp.arange(T, dtype=jnp.int32)
        att_s = dsa_attend(q2, qi2, wi2, qpos_s, k_all, v_all, ki_all, k_sel_s).reshape(Bd, T, D_ATT)
        y_sample = merge_and_norm(x_sample, y_ssd_s, att_s, z_a2, w_out, ln_g, ln_b)
    return (y_prompt, y_sample, k_p, v_p, ki_p, ssm_p, conv_p, k_s, v_s, ki_s, ssm_s, conv_s)
```

```python
import functools

import numpy as np
import jax
import jax.numpy as jnp
from jax import lax
from jax.experimental import pallas as pl
from jax.experimental.pallas import tpu as pltpu

F32 = jnp.float32
BF16 = jnp.bfloat16
I32 = jnp.int32
HIGHEST = lax.Precision.HIGHEST

D_MODEL = 1024
CHUNK = 64
D_SSM = 1024
SSD_HEAD_DIM = 64
N_SSD_HEADS = 16
N_SSD_GROUPS = 2
D_STATE = 128
CONV_W = 4
CONV_DIM = D_SSM + 2 * N_SSD_GROUPS * D_STATE
D_ATT = 1024
ATT_HEAD_DIM = 128
N_ATT_HEADS = 8
N_KV_HEADS = 2
KV_DIM = N_KV_HEADS * ATT_HEAD_DIM
IDX_HEADS = 8
IDX_DIM = 64
TOPK_MAX = 256
ALPHA = 2.0 ** 0.25
LN_EPS = 1e-5
RMS_EPS = 1e-5

_SRC = dict(zs=0, xbc=1024, dt=2560, q=2576, k=3600, v=3856, za=4112, qi=5136, ki=5648, wi=5712)
OFF_ZS, OFF_Q, OFF_ZA, OFF_XBC, OFF_QI, OFF_K, OFF_V, OFF_SMALL = 0, 1024, 2048, 3072, 4608, 5120, 5376, 5632
D_PAD = 5760
SM_DT, SM_KI, SM_WI = 0, 16, 80

INT_MIN = -(2 ** 31)
NEG_BIG = -1e30

VMEM_LIMIT = 56 * 1024 * 1024


def _dot(a, b, dims=(((1,), (0,)), ((), ())), precision=None):
    return lax.dot_general(a, b, dims, precision=precision, preferred_element_type=F32)


_NT = (((1,), (1,)), ((), ()))
_TN = (((0,), (0,)), ((), ()))


def _silu(x):
    return x * (1.0 / (1.0 + jnp.exp(-x)))


def _softplus(x):
    return jnp.maximum(x, 0.0) + jnp.log1p(jnp.exp(-jnp.abs(x)))


def _float_key(x):
    b = pltpu.bitcast(x, I32)
    return b ^ ((b >> 31) & jnp.int32(0x7FFFFFFF))


_PROJ_CHUNKS = tuple((c, 512) for c in range(0, 5632, 512)) + ((5632, 128),)


def _proj_kernel(x_ref, w_ref, h_ref, k_ref, v_ref, ki_ref):
    xb = x_ref[...].astype(BF16)
    for c0, cw in _PROJ_CHUNKS:
        h_ref[:, c0:c0 + cw] = _dot(xb, w_ref[:, c0:c0 + cw])
    k_ref[...] = h_ref[:, OFF_K:OFF_K + KV_DIM]
    v_ref[...] = h_ref[:, OFF_V:OFF_V + KV_DIM]
    ki_ref[...] = h_ref[:, OFF_SMALL + SM_KI:OFF_SMALL + SM_KI + IDX_DIM]


def _project(x2d, w_pad, tm):
    m = x2d.shape[0]
    assert m % tm == 0
    return pl.pallas_call(
        _proj_kernel,
        grid=(m // tm,),
        in_specs=[
            pl.BlockSpec((tm, D_MODEL), lambda i: (i, 0)),
            pl.BlockSpec((D_MODEL, D_PAD), lambda i: (0, 0), pipeline_mode=pl.Buffered(1)),
        ],
        out_specs=[
            pl.BlockSpec((tm, D_PAD), lambda i: (i, 0)),
            pl.BlockSpec((tm, KV_DIM), lambda i: (i, 0)),
            pl.BlockSpec((tm, KV_DIM), lambda i: (i, 0)),
            pl.BlockSpec((tm, IDX_DIM), lambda i: (i, 0)),
        ],
        out_shape=[
            jax.ShapeDtypeStruct((m, D_PAD), F32),
            jax.ShapeDtypeStruct((m, KV_DIM), F32),
            jax.ShapeDtypeStruct((m, KV_DIM), F32),
            jax.ShapeDtypeStruct((m, IDX_DIM), F32),
        ],
        compiler_params=pltpu.CompilerParams(
            dimension_semantics=("arbitrary",), vmem_limit_bytes=VMEM_LIMIT),
        name="proj",
    )(x2d, w_pad)


def _ssd_kernel(zs_ref, xbc_ref, sm_ref, cprev_ref, sprev_ref, cw_ref, cb_ref, dtb_ref, alog_ref,
                dskip_ref, nw_ref, y_ref, snew_ref, xpad, xc_s, state, *, Q, Lb):
    i = pl.program_id(1)
    nblk = pl.num_programs(1)
    H, P, N = N_SSD_HEADS, SSD_HEAD_DIM, D_STATE
    HQ = H * Q
    hp = min(H, 256 // Q)
    n_diag = H // hp
    GP = (H // N_SSD_GROUPS) * P

    @pl.when(i == 0)
    def _():
        xpad[0:8, :] = cprev_ref[0]
        state[...] = sprev_ref[0]

    @pl.when(i > 0)
    def _():
        xpad[0:8, :] = xpad[Lb:Lb + 8, :]

    xpad[8:8 + Lb, :] = xbc_ref[...]
    conv = cb_ref[...] + xpad[5:5 + Lb, :] * cw_ref[0:1, :]
    conv = conv + xpad[6:6 + Lb, :] * cw_ref[1:2, :]
    conv = conv + xpad[7:7 + Lb, :] * cw_ref[2:3, :]
    conv = conv + xpad[8:8 + Lb, :] * cw_ref[3:4, :]
    xc_s[...] = _silu(conv)

    a_neg = -jnp.exp(alog_ref[...])

    def iota(shape, d):
        return lax.broadcasted_iota(I32, shape, d)

    lq = Q.bit_length() - 1
    e_p = (iota((H, H * P), 1) >> 6 == iota((H, H * P), 0)).astype(F32)
    e_q = (iota((H, HQ), 1) >> lq == iota((H, HQ), 0)).astype(F32)
    j_of = iota((Q, HQ), 1) & (Q - 1)
    t_of = iota((Q, HQ), 0)
    u_rep = (t_of <= j_of).astype(F32)
    causal = j_of <= t_of
    c2 = iota((Q, 2 * Q), 1)
    r2 = iota((Q, 2 * Q), 0)
    seg_lhs = jnp.where(c2 < Q, (c2 <= r2).astype(F32), -1.0)
    tril = (iota((Q, Q), 1) <= iota((Q, Q), 0)).astype(F32)
    ones_qn = jnp.ones((Q, N), F32)
    bd_mask = (iota((hp * Q, hp * P), 0) >> lq) == (iota((hp * Q, hp * P), 1) >> 6)

    def chunk(c, carry):
        r0 = pl.multiple_of(c * Q, Q)
        dt = _softplus(sm_ref[pl.ds(r0, Q), SM_DT:SM_DT + H] + dtb_ref[...])
        a = dt * a_neg
        both = _dot(jnp.concatenate([dt, a], axis=0), e_p, precision=HIGHEST)
        dt_full, a_full = both[:Q], both[Q:]
        acum_full = _dot(tril, a_full, precision=HIGHEST)
        tot_full = acum_full[Q - 1:Q, :]
        xs = xc_s[pl.ds(r0, Q), 0:D_SSM]
        bm = xc_s[pl.ds(r0, Q), D_SSM:D_SSM + N_SSD_GROUPS * N].astype(BF16)
        cm = xc_s[pl.ds(r0, Q), D_SSM + N_SSD_GROUPS * N:CONV_DIM].astype(BF16)
        xd = xs * dt_full
        xdw = (xd * jnp.exp(tot_full - acum_full)).astype(BF16)
        xdb = xd.astype(BF16)

        a_q = _dot(a, e_q, precision=HIGHEST)
        seg = _dot(seg_lhs, jnp.concatenate([a_q, a_q * u_rep], axis=0), precision=HIGHEST)
        decay = jnp.exp(jnp.where(causal, seg, -jnp.inf))
        cb_parts = []
        for g in range(N_SSD_GROUPS):
            bm_g = bm[:, g * N:(g + 1) * N]
            rep = jnp.concatenate([bm_g] * (H // N_SSD_GROUPS), axis=0)
            cb_parts.append(_dot(cm[:, g * N:(g + 1) * N], rep, _NT))
        mmat = (jnp.concatenate(cb_parts, axis=1) * decay).astype(BF16)

        y_parts = []
        for d in range(n_diag):
            xd_d = xdb[:, d * hp * P:(d + 1) * hp * P]
            bd = jnp.where(bd_mask, jnp.concatenate([xd_d] * hp, axis=0), jnp.zeros((), BF16))
            y_parts.append(_dot(mmat[:, d * hp * Q:(d + 1) * hp * Q], bd))
        y_diag = jnp.concatenate(y_parts, axis=1) if n_diag > 1 else y_parts[0]

        totcol = _dot(a_full, ones_qn, _TN, precision=HIGHEST)
        y_off_parts = []
        for g in range(N_SSD_GROUPS):
            st_g = state[g * GP:(g + 1) * GP, :]
            y_off_parts.append(_dot(cm[:, g * N:(g + 1) * N], st_g.astype(BF16), _NT))
            upd = _dot(xdw[:, g * GP:(g + 1) * GP], bm[:, g * N:(g + 1) * N], _TN)
            state[g * GP:(g + 1) * GP, :] = jnp.exp(totcol[g * GP:(g + 1) * GP, :]) * st_g + upd
        y_off = jnp.concatenate(y_off_parts, axis=1) * jnp.exp(acum_full)

        y = (y_diag + y_off) + dskip_ref[...] * xs
        gt = y * _silu(zs_ref[pl.ds(r0, Q), :])
        for g in range(N_SSD_GROUPS):
            gg = gt[:, g * GP:(g + 1) * GP]
            ms = jnp.mean(gg * gg, axis=-1, keepdims=True)
            y_ref[pl.ds(r0, Q), g * GP:(g + 1) * GP] = (
                gg * lax.rsqrt(ms + RMS_EPS) * nw_ref[:, g * GP:(g + 1) * GP])
        return carry

    lax.fori_loop(0, Lb // Q, chunk, 0)

    @pl.when(i == nblk - 1)
    def _():
        snew_ref[0] = state[...]


def _ssd(h2d, conv_prev8, ssm_prev, conv_w, conv_b, dt_bias, a_log, dskip_full, norm_w, *, B, L, Q, Lb):
    assert L % Lb == 0 and Lb % Q == 0 and Lb % 8 == 0
    nblk = L // Lb
    HP = N_SSD_HEADS * SSD_HEAD_DIM
    row = lambda b, i: b * nblk + i
    full2 = lambda shp: pl.BlockSpec(shp, lambda b, i: (0, 0))
    return pl.pallas_call(
        functools.partial(_ssd_kernel, Q=Q, Lb=Lb),
        grid=(B, nblk),
        in_specs=[
            pl.BlockSpec((Lb, D_SSM), lambda b, i: (row(b, i), OFF_ZS // D_SSM)),
            pl.BlockSpec((Lb, CONV_DIM), lambda b, i: (row(b, i), OFF_XBC // CONV_DIM)),
            pl.BlockSpec((Lb, 128), lambda b, i: (row(b, i), OFF_SMALL // 128)),
            pl.BlockSpec((1, 8, CONV_DIM), lambda b, i: (b, 0, 0)),
            pl.BlockSpec((1, HP, D_STATE), lambda b, i: (b, 0, 0)),
            full2((CONV_W, CONV_DIM)),
            full2((1, CONV_DIM)),
            full2((1, N_SSD_HEADS)),
            full2((1, N_SSD_HEADS)),
            full2((1, D_SSM)),
            full2((1, D_SSM)),
        ],
        out_specs=[
            pl.BlockSpec((Lb, D_SSM), lambda b, i: (row(b, i), 0)),
            pl.BlockSpec((1, HP, D_STATE), lambda b, i: (b, 0, 0)),
        ],
        out_shape=[
            jax.ShapeDtypeStruct((B * L, D_SSM), F32),
            jax.ShapeDtypeStruct((B, HP, D_STATE), F32),
        ],
        scratch_shapes=[
            pltpu.VMEM((Lb + 8, CONV_DIM), F32),
            pltpu.VMEM((Lb, CONV_DIM), F32),
            pltpu.VMEM((HP, D_STATE), F32),
        ],
        compiler_params=pltpu.CompilerParams(
            dimension_semantics=("arbitrary", "arbitrary"), vmem_limit_bytes=VMEM_LIMIT),
        name="ssd",
    )(h2d, h2d, h2d, conv_prev8, ssm_prev, conv_w, conv_b, dt_bias, a_log, dskip_full, norm_w)


def _alibi_slope(h):
    return float(2.0 ** (-8.0 * (h + 1) / N_ATT_HEADS))


def _dsa_prompt_kernel(q_ref, qi_ref, sm_ref, kb_ref, vt_ref, kib_ref, o_ref,
                       keys_s, m_s, l_s, acc_s, *, Qb, k_sel):
    Kb = Qb
    i = pl.program_id(0)
    nk = i + 1
    rpg = N_ATT_HEADS // N_KV_HEADS

    q_t = (q_ref[...].T * (ATT_HEAD_DIM ** -0.5)).astype(BF16)
    qi_t = qi_ref[...].T.astype(BF16)
    w_t = sm_ref[...].T[SM_WI:SM_WI + IDX_HEADS, :] * (IDX_DIM ** -0.5 * IDX_HEADS ** -0.5)
    qpos = i * Qb + lax.broadcasted_iota(I32, (1, Qb), 1)
    chunk_end = (((qpos >> 6) + 1) << 6) - 1
    krow = lax.broadcasted_iota(I32, (Kb, 1), 0)

    def score_blk(j, carry):
        kib = kib_ref[j]
        score = jnp.zeros((Kb, Qb), F32)
        for h in range(IDX_HEADS):
            s = _dot(kib, qi_t[h * IDX_DIM:(h + 1) * IDX_DIM, :])
            score = score + w_t[h:h + 1, :] * jnp.maximum(s, 0.0)
        adm = (j * Kb + krow) <= chunk_end
        keys_s[j] = jnp.where(adm, _float_key(score), INT_MIN)
        return carry

    lax.fori_loop(0, nk, score_blk, 0)

    def count(pred):
        def body(j, acc):
            hit = jnp.where(pred(keys_s[j]), 1, 0).astype(I32)
            return acc + hit.reshape(Kb // 8, 8, Qb).sum(axis=0)
        acc = lax.fori_loop(0, nk, body, jnp.zeros((8, Qb), I32))
        return acc.sum(axis=0, keepdims=True)

    def bit_step(t, prefix):
        cand_u = prefix | lax.shift_left(jnp.int32(1), 31 - t)
        cand = cand_u ^ INT_MIN
        cnt = count(lambda kk: kk >= cand)
        return jnp.where(cnt >= k_sel, cand_u, prefix)

    thr = lax.fori_loop(0, 32, bit_step, jnp.zeros((1, Qb), I32)) ^ INT_MIN
    n_gt = count(lambda kk: kk > thr)
    need = jnp.where(thr == INT_MIN, 0, k_sel - n_gt).astype(F32)

    m_s[...] = jnp.full(m_s.shape, NEG_BIG, F32)
    l_s[...] = jnp.zeros(l_s.shape, F32)
    acc_s[...] = jnp.zeros(acc_s.shape, F32)
    tril = (lax.broadcasted_iota(I32, (Kb, Kb), 1) <= lax.broadcasted_iota(I32, (Kb, Kb), 0)).astype(BF16)

    def attn_blk(j, seen_eq):
        keys = keys_s[j]
        eq = keys == thr
        rank = seen_eq + _dot(tril, jnp.where(eq, 1.0, 0.0).astype(BF16))
        sel = (keys > thr) | (eq & (rank <= need))
        dist = jnp.abs(qpos - (j * Kb + krow)).astype(F32)
        kblk = kb_ref[j]
        vblk = vt_ref[j]
        for h in range(N_ATT_HEADS):
            g = h // rpg
            lg = _dot(kblk[:, g * ATT_HEAD_DIM:(g + 1) * ATT_HEAD_DIM],
                      q_t[h * ATT_HEAD_DIM:(h + 1) * ATT_HEAD_DIM, :])
            lg = jnp.where(sel, lg - _alibi_slope(h) * dist, -jnp.inf)
            m_old = m_s[h:h + 1, :]
            m_new = jnp.maximum(m_old, jnp.max(lg, axis=0, keepdims=True))
            p = jnp.exp(lg - m_new)
            alpha = jnp.exp(m_old - m_new)
            l_s[h:h + 1, :] = alpha * l_s[h:h + 1, :] + jnp.sum(p, axis=0, keepdims=True)
            acc_s[h] = alpha * acc_s[h] + _dot(vblk[g * ATT_HEAD_DIM:(g + 1) * ATT_HEAD_DIM, :], p.astype(BF16))
            m_s[h:h + 1, :] = m_new
        return rank[Kb - 1:Kb, :]

    lax.fori_loop(0, nk, attn_blk, jnp.zeros((1, Qb), F32))

    for h in range(N_ATT_HEADS):
        out_t = acc_s[h] * (1.0 / l_s[h:h + 1, :])
        o_ref[:, h * ATT_HEAD_DIM:(h + 1) * ATT_HEAD_DIM] = out_t.T


def _dsa_prompt(h2d, kb3, vt3, kib3, *, S, Qb, k_sel):
    nkb = S // Qb
    const3 = lambda shp: pl.BlockSpec(shp, lambda i: (0, 0, 0), pipeline_mode=pl.Buffered(1))
    return pl.pallas_call(
        functools.partial(_dsa_prompt_kernel, Qb=Qb, k_sel=k_sel),
        grid=(nkb,),
        in_specs=[
            pl.BlockSpec((Qb, D_ATT), lambda i: (i, OFF_Q // D_ATT)),
            pl.BlockSpec((Qb, IDX_HEADS * IDX_DIM), lambda i: (i, OFF_QI // (IDX_HEADS * IDX_DIM))),
            pl.BlockSpec((Qb, 128), lambda i: (i, OFF_SMALL // 128)),
            const3((nkb, Qb, KV_DIM)),
            const3((nkb, KV_DIM, Qb)),
            const3((nkb, Qb, IDX_DIM)),
        ],
        out_specs=pl.BlockSpec((Qb, D_ATT), lambda i: (i, 0)),
        out_shape=jax.ShapeDtypeStruct((S, D_ATT), F32),
        scratch_shapes=[
            pltpu.VMEM((nkb, Qb, Qb), I32),
            pltpu.VMEM((N_ATT_HEADS, Qb), F32),
            pltpu.VMEM((N_ATT_HEADS, Qb), F32),
            pltpu.VMEM((N_ATT_HEADS, ATT_HEAD_DIM, Qb), F32),
        ],
        compiler_params=pltpu.CompilerParams(
            dimension_semantics=("arbitrary",), vmem_limit_bytes=VMEM_LIMIT),
        name="dsa_prompt",
    )(h2d, h2d, h2d, kb3, vt3, kib3)


def _dsa_sample_kernel(qr_ref, qir_ref, wr_ref, slope_ref, ck_ref, cv_ref, cki_ref, kn_ref, vn_ref, kin_ref,
                       o_ref, keys_s, sel_s, *, T, P, k_sel):
    LN = 128
    L = P + LN
    rpg = N_ATT_HEADS // N_KV_HEADS
    R = rpg * T

    qi_b = qir_ref[0].astype(BF16)
    w_col = wr_ref[0] * (IDX_DIM ** -0.5 * IDX_HEADS ** -0.5)

    def head_sum(s):
        return (jnp.maximum(s, 0.0) * w_col).reshape(IDX_HEADS, T, s.shape[-1]).sum(axis=0)

    sc_c = head_sum(_dot(qi_b, cki_ref[0].astype(BF16), _NT))
    sc_n = head_sum(_dot(qi_b, kin_ref[0].astype(BF16), _NT))
    keys_s[:, 0:P] = _float_key(sc_c)
    col_n = lax.broadcasted_iota(I32, (T, LN), 1)
    keys_s[:, P:L] = jnp.where(col_n < T, _float_key(sc_n), INT_MIN)

    def count(pred):
        return jnp.sum(jnp.where(pred(keys_s[...]), 1, 0).astype(I32), axis=1, keepdims=True)

    def bit_step(t, prefix):
        cand_u = prefix | lax.shift_left(jnp.int32(1), 31 - t)
        cand = cand_u ^ INT_MIN
        return jnp.where(count(lambda kk: kk >= cand) >= k_sel, cand_u, prefix)

    thr = lax.fori_loop(0, 32, bit_step, jnp.zeros((T, 1), I32)) ^ INT_MIN
    n_gt = count(lambda kk: kk > thr)
    need = jnp.where(thr == INT_MIN, 0, k_sel - n_gt).astype(F32)

    triu = (lax.broadcasted_iota(I32, (LN, LN), 0) <= lax.broadcasted_iota(I32, (LN, LN), 1)).astype(BF16)
    seen = jnp.zeros((T, 1), F32)
    for jb in range(L // LN):
        kk = keys_s[:, jb * LN:(jb + 1) * LN]
        eq = kk == thr
        rank = seen + _dot(jnp.where(eq, 1.0, 0.0).astype(BF16), triu)
        sel_s[:, jb * LN:(jb + 1) * LN] = jnp.where((kk > thr) | (eq & (rank <= need)), 1.0, 0.0)
        seen = rank[:, LN - 1:LN]

    qpos = P + lax.broadcasted_iota(I32, (T, L), 0)
    kpos = lax.broadcasted_iota(I32, (T, L), 1)
    dist = jnp.abs(qpos - kpos).astype(F32)
    dist_r = jnp.concatenate([dist] * rpg, axis=0)
    sel_r = jnp.concatenate([sel_s[...]] * rpg, axis=0) > 0.5
    for g in range(N_KV_HEADS):
        sl = slice(g * ATT_HEAD_DIM, (g + 1) * ATT_HEAD_DIM)
        qg = (qr_ref[0, g * R:(g + 1) * R, :] * (ATT_HEAD_DIM ** -0.5)).astype(BF16)
        lg = jnp.concatenate([_dot(qg, ck_ref[0, :, sl].astype(BF16), _NT),
                              _dot(qg, kn_ref[0, :, sl].astype(BF16), _NT)], axis=1)
        lg = jnp.where(sel_r, lg - slope_ref[g * R:(g + 1) * R, :] * dist_r, -jnp.inf)
        m = jnp.max(lg, axis=1, keepdims=True)
        p = jnp.exp(lg - m)
        den = jnp.sum(p, axis=1, keepdims=True)
        pb = p.astype(BF16)
        out = _dot(pb[:, 0:P], cv_ref[0, :, sl].astype(BF16)) + _dot(pb[:, P:L], vn_ref[0, :, sl].astype(BF16))
        out = out * (1.0 / den)
        for r in range(rpg):
            hh = g * rpg + r
            o_ref[0, :, hh * ATT_HEAD_DIM:(hh + 1) * ATT_HEAD_DIM] = out[r * T:(r + 1) * T, :]


def _dsa_sample(q_rows, qi_rows, w_rows, slope_rows, ck, cv, cki, kn, vn, kin, *, Bd, T, P, k_sel):
    LN = 128
    per_b = lambda shp: pl.BlockSpec(shp, lambda b: (b, 0, 0))
    return pl.pallas_call(
        functools.partial(_dsa_sample_kernel, T=T, P=P, k_sel=k_sel),
        grid=(Bd,),
        in_specs=[
            per_b((1, N_ATT_HEADS * T, ATT_HEAD_DIM)),
            per_b((1, IDX_HEADS * T, IDX_DIM)),
            per_b((1, IDX_HEADS * T, 1)),
            pl.BlockSpec((N_ATT_HEADS * T, 1), lambda b: (0, 0)),
            per_b((1, P, KV_DIM)),
            per_b((1, P, KV_DIM)),
            per_b((1, P, IDX_DIM)),
            per_b((1, LN, KV_DIM)),
            per_b((1, LN, KV_DIM)),
            per_b((1, LN, IDX_DIM)),
        ],
        out_specs=per_b((1, T, D_ATT)),
        out_shape=jax.ShapeDtypeStruct((Bd, T, D_ATT), F32),
        scratch_shapes=[
            pltpu.VMEM((T, P + LN), I32),
            pltpu.VMEM((T, P + LN), F32),
        ],
        compiler_params=pltpu.CompilerParams(
            dimension_semantics=("arbitrary",), vmem_limit_bytes=VMEM_LIMIT),
        name="dsa_sample",
    )(q_rows, qi_rows, w_rows, slope_rows, ck, cv, cki, kn, vn, kin)


def _merge_kernel(x_ref, ys_ref, att_ref, za_ref, wo_ref, g_ref, b_ref, o_ref):
    att = att_ref[...] * _silu(za_ref[...])
    mix = _dot(ys_ref[...].astype(BF16), wo_ref[0:D_SSM, :]) + _dot(att.astype(BF16), wo_ref[D_SSM:, :])
    hres = ALPHA * x_ref[...] + mix
    mu = jnp.mean(hres, axis=-1, keepdims=True)
    hc = hres - mu
    var = jnp.mean(hc * hc, axis=-1, keepdims=True)
    o_ref[...] = hc * lax.rsqrt(var + LN_EPS) * g_ref[...] + b_ref[...]


def _merge(x2d, y_ssd, att, h2d, w_out_b, ln_g, ln_b, tm):
    m = x2d.shape[0]
    rows = lambda i: (i, 0)
    return pl.pallas_call(
        _merge_kernel,
        grid=(m // tm,),
        in_specs=[
            pl.BlockSpec((tm, D_MODEL), rows),
            pl.BlockSpec((tm, D_SSM), rows),
            pl.BlockSpec((tm, D_ATT), rows),
            pl.BlockSpec((tm, D_ATT), lambda i: (i, OFF_ZA // D_ATT)),
            pl.BlockSpec((D_SSM + D_ATT, D_MODEL), lambda i: (0, 0), pipeline_mode=pl.Buffered(1)),
            pl.BlockSpec((1, D_MODEL), lambda i: (0, 0)),
            pl.BlockSpec((1, D_MODEL), lambda i: (0, 0)),
        ],
        out_specs=pl.BlockSpec((tm, D_MODEL), rows),
        out_shape=jax.ShapeDtypeStruct((m, D_MODEL), F32),
        compiler_params=pltpu.CompilerParams(
            dimension_semantics=("arbitrary",), vmem_limit_bytes=VMEM_LIMIT),
        name="merge",
    )(x2d, y_ssd, att, h2d, w_out_b, ln_g, ln_b)


def _regroup_w_in(w_in):
    s = _SRC
    cols = [w_in[:, s["zs"]:s["zs"] + 1024], w_in[:, s["q"]:s["q"] + 1024], w_in[:, s["za"]:s["za"] + 1024],
            w_in[:, s["xbc"]:s["xbc"] + CONV_DIM], w_in[:, s["qi"]:s["qi"] + 512],
            w_in[:, s["k"]:s["k"] + KV_DIM], w_in[:, s["v"]:s["v"] + KV_DIM],
            w_in[:, s["dt"]:s["dt"] + 16], w_in[:, s["ki"]:s["ki"] + IDX_DIM], w_in[:, s["wi"]:s["wi"] + IDX_HEADS],
            jnp.zeros((D_MODEL, 128 - 16 - IDX_DIM - IDX_HEADS), w_in.dtype)]
    return jnp.concatenate(cols, axis=1).astype(BF16)


def _row_tile(m):
    return 256 if m % 256 == 0 else m


def kernel(x_prompt, x_sample, cache_k, cache_v, cache_kidx, state_ssm, state_conv, w_in, conv_w, conv_b,
           dt_bias, a_log, d_skip, ssd_norm_w, w_out, ln_g, ln_b):
    B, S, _ = x_prompt.shape
    Bd, T, _ = x_sample.shape
    P = cache_k.shape[1]
    assert B == 1 and S % 256 == 0 and T % 8 == 0 and T >= CONV_W - 1 and T <= 128 and P % 128 == 0
    HP = N_SSD_HEADS * SSD_HEAD_DIM

    w_pad = _regroup_w_in(w_in)
    w_out_b = w_out.astype(BF16)
    conv_b2 = conv_b.reshape(1, CONV_DIM)
    dtb2 = dt_bias.reshape(1, N_SSD_HEADS)
    alog2 = a_log.reshape(1, N_SSD_HEADS)
    dskip_full = jnp.repeat(d_skip, SSD_HEAD_DIM).reshape(1, D_SSM)
    nw2 = ssd_norm_w.reshape(1, D_SSM)
    g2 = ln_g.reshape(1, D_MODEL)
    b2 = ln_b.reshape(1, D_MODEL)

    xp = x_prompt.reshape(S, D_MODEL)
    h_p, k_p, v_p, ki_p = _project(xp, w_pad, _row_tile(S))
    conv0 = jnp.zeros((1, 8, CONV_DIM), F32)
    ssm0 = jnp.zeros((1, HP, D_STATE), F32)
    yssd_p, ssm_p = _ssd(h_p, conv0, ssm0, conv_w, conv_b2, dtb2, alog2, dskip_full, nw2,
                         B=1, L=S, Q=CHUNK, Lb=256)
    Qb = 256
    nkb = S // Qb
    kb3 = k_p.astype(BF16).reshape(nkb, Qb, KV_DIM)
    vt3 = v_p.astype(BF16).reshape(nkb, Qb, KV_DIM).transpose(0, 2, 1)
    kib3 = ki_p.astype(BF16).reshape(nkb, Qb, IDX_DIM)
    att_p = _dsa_prompt(h_p, kb3, vt3, kib3, S=S, Qb=Qb, k_sel=min(TOPK_MAX, S // 4))
    y_p = _merge(xp, yssd_p, att_p, h_p, w_out_b, g2, b2, _row_tile(S))

    M = Bd * T
    xs = x_sample.reshape(M, D_MODEL)
    h_s, k_s, v_s, ki_s = _project(xs, w_pad, _row_tile(M))
    conv_prev8 = jnp.pad(state_conv, ((0, 0), (8 - (CONV_W - 1), 0), (0, 0)))
    yssd_s, ssm_s = _ssd(h_s, conv_prev8, state_ssm.reshape(Bd, HP, D_STATE), conv_w, conv_b2, dtb2, alog2,
                         dskip_full, nw2, B=Bd, L=T, Q=T, Lb=T)
    h_s3 = h_s.reshape(Bd, T, D_PAD)

    def head_rows(a, nh, dh):
        return a.reshape(Bd, T, nh, dh).transpose(0, 2, 1, 3).reshape(Bd, nh * T, dh)

    q_rows = head_rows(h_s3[:, :, OFF_Q:OFF_Q + D_ATT], N_ATT_HEADS, ATT_HEAD_DIM)
    qi_rows = head_rows(h_s3[:, :, OFF_QI:OFF_QI + IDX_HEADS * IDX_DIM], IDX_HEADS, IDX_DIM)
    w_rows = head_rows(h_s3[:, :, OFF_SMALL + SM_WI:OFF_SMALL + SM_WI + IDX_HEADS], IDX_HEADS, 1)
    slope_rows = jnp.asarray(np.repeat([_alibi_slope(h) for h in range(N_ATT_HEADS)], T).reshape(-1, 1), F32)
    padn = lambda a: jnp.pad(a.reshape(Bd, T, -1), ((0, 0), (0, 128 - T), (0, 0)))
    att_s = _dsa_sample(q_rows, qi_rows, w_rows, slope_rows,
                        cache_k.reshape(Bd, P, KV_DIM), cache_v.reshape(Bd, P, KV_DIM), cache_kidx,
                        padn(k_s), padn(v_s), padn(ki_s), Bd=Bd, T=T, P=P, k_sel=min(TOPK_MAX, (P + T) // 4))
    y_s = _merge(xs, yssd_s, att_s.reshape(M, D_ATT), h_s, w_out_b, g2, b2, _row_tile(M))

    kv4 = lambda a, b_, l: a.reshape(b_, l, N_KV_HEADS, ATT_HEAD_DIM)
    st4 = lambda a, b_: a.reshape(b_, N_SSD_HEADS, SSD_HEAD_DIM, D_STATE)
    conv_p = h_p[S - (CONV_W - 1):, OFF_XBC:OFF_XBC + CONV_DIM].reshape(1, CONV_W - 1, CONV_DIM)
    conv_s = h_s3[:, T - (CONV_W - 1):, OFF_XBC:OFF_XBC + CONV_DIM]
    return (y_p.reshape(1, S, D_MODEL), y_s.reshape(Bd, T, D_MODEL),
            kv4(k_p, 1, S), kv4(v_p, 1, S), ki_p.reshape(1, S, IDX_DIM), st4(ssm_p, 1), conv_p,
            kv4(k_s, Bd, T), kv4(v_s, Bd, T), ki_s.reshape(Bd, T, IDX_DIM), st4(ssm_s, Bd), conv_s)
```

```python
import functools

import numpy as np
import jax
import jax.numpy as jnp
from jax import lax
from jax.experimental import pallas as pl
from jax.experimental.pallas import tpu as pltpu

F32 = jnp.float32
BF16 = jnp.bfloat16
I32 = jnp.int32
HIGHEST = lax.Precision.HIGHEST

D_MODEL = 1024
CHUNK = 64
D_SSM = 1024
SSD_HEAD_DIM = 64
N_SSD_HEADS = 16
N_SSD_GROUPS = 2
D_STATE = 128
CONV_W = 4
CONV_DIM = D_SSM + 2 * N_SSD_GROUPS * D_STATE
D_ATT = 1024
ATT_HEAD_DIM = 128
N_ATT_HEADS = 8
N_KV_HEADS = 2
KV_DIM = N_KV_HEADS * ATT_HEAD_DIM
IDX_HEADS = 8
IDX_DIM = 64
TOPK_MAX = 256
ALPHA = 2.0 ** 0.25
LN_EPS = 1e-5
RMS_EPS = 1e-5

_SRC = dict(zs=0, xbc=1024, dt=2560, q=2576, k=3600, v=3856, za=4112, qi=5136, ki=5648, wi=5712)
OFF_ZS, OFF_Q, OFF_ZA, OFF_XBC, OFF_QI, OFF_K, OFF_V, OFF_SMALL = 0, 1024, 2048, 3072, 4608, 5120, 5376, 5632
D_PAD = 5760
SM_DT, SM_KI, SM_WI = 0, 16, 80

INT_MIN = -(2 ** 31)
NEG_BIG = -1e30

VMEM_LIMIT = 56 * 1024 * 1024


def _dot(a, b, dims=(((1,), (0,)), ((), ())), precision=None):
    return lax.dot_general(a, b, dims, precision=precision, preferred_element_type=F32)


_NT = (((1,), (1,)), ((), ()))
_TN = (((0,), (0,)), ((), ()))


def _silu(x):
    return x * (1.0 / (1.0 + jnp.exp(-x)))


def _softplus(x):
    return jnp.maximum(x, 0.0) + jnp.log1p(jnp.exp(-jnp.abs(x)))


def _pattern_to_float(u):
    key = u ^ INT_MIN
    return pltpu.bitcast(key ^ ((key >> 31) & jnp.int32(0x7FFFFFFF)), F32)


def _kth_largest(count, k_sel, shape):
    def bit_step(t, carry):
        prefix, n_ge = carry
        cand_u = prefix | lax.shift_left(jnp.int32(1), 31 - t)
        cnt = count(lambda s: s >= _pattern_to_float(cand_u))
        ok = cnt >= k_sel
        return jnp.where(ok, cand_u, prefix), jnp.where(ok, cnt, n_ge)

    n_valid = count(lambda s: s > -jnp.inf)
    prefix, n_ge = lax.fori_loop(0, 32, bit_step, (jnp.zeros(shape, I32), n_valid))
    few = n_valid < k_sel
    thr = jnp.where(few, jnp.float32(np.finfo(np.float32).min), _pattern_to_float(prefix))
    return thr, jnp.where(few, k_sel, n_ge), few


def _split_bf16x3(x):
    parts, r = [], np.float32(x)
    for _ in range(3):
        p = np.float32(np.asarray(r, dtype=BF16))
        parts.append(float(p))
        r = np.float32(r - p)
    assert r == 0.0
    return parts


_PROJ_CHUNKS = tuple((c, 512) for c in range(0, 5632, 512)) + ((5632, 128),)


def _proj_kernel(x_ref, w_ref, h_ref, k_ref, v_ref, ki_ref):
    xb = x_ref[...].astype(BF16)
    for c0, cw in _PROJ_CHUNKS:
        h_ref[:, c0:c0 + cw] = _dot(xb, w_ref[:, c0:c0 + cw])
    k_ref[...] = h_ref[:, OFF_K:OFF_K + KV_DIM]
    v_ref[...] = h_ref[:, OFF_V:OFF_V + KV_DIM]
    ki_ref[...] = h_ref[:, OFF_SMALL + SM_KI:OFF_SMALL + SM_KI + IDX_DIM]


def _project(x2d, w_pad, tm):
    m = x2d.shape[0]
    assert m % tm == 0
    return pl.pallas_call(
        _proj_kernel,
        grid=(m // tm,),
        in_specs=[
            pl.BlockSpec((tm, D_MODEL), lambda i: (i, 0)),
            pl.BlockSpec((D_MODEL, D_PAD), lambda i: (0, 0), pipeline_mode=pl.Buffered(1)),
        ],
        out_specs=[
            pl.BlockSpec((tm, D_PAD), lambda i: (i, 0)),
            pl.BlockSpec((tm, KV_DIM), lambda i: (i, 0)),
            pl.BlockSpec((tm, KV_DIM), lambda i: (i, 0)),
            pl.BlockSpec((tm, IDX_DIM), lambda i: (i, 0)),
        ],
        out_shape=[
            jax.ShapeDtypeStruct((m, D_PAD), F32),
            jax.ShapeDtypeStruct((m, KV_DIM), F32),
            jax.ShapeDtypeStruct((m, KV_DIM), F32),
            jax.ShapeDtypeStruct((m, IDX_DIM), F32),
        ],
        compiler_params=pltpu.CompilerParams(
            dimension_semantics=("arbitrary",), vmem_limit_bytes=VMEM_LIMIT),
        name="proj",
    )(x2d, w_pad)


def _ssd_kernel(zs_ref, xbc_ref, sm_ref, cprev_ref, sprev_ref, cw_ref, cb_ref, dtb_ref, alog_ref,
                dskip_ref, nw_ref, y_ref, snew_ref, xpad, xc_s, state, *, Q, Lb):
    i = pl.program_id(1)
    nblk = pl.num_programs(1)
    H, P, N = N_SSD_HEADS, SSD_HEAD_DIM, D_STATE
    HQ = H * Q
    hp = min(H, 256 // Q)
    n_diag = H // hp
    GP = (H // N_SSD_GROUPS) * P

    @pl.when(i == 0)
    def _():
        xpad[0:8, :] = cprev_ref[0]
        state[...] = sprev_ref[0]

    @pl.when(i > 0)
    def _():
        xpad[0:8, :] = xpad[Lb:Lb + 8, :]

    xpad[8:8 + Lb, :] = xbc_ref[...]
    conv = cb_ref[...] + xpad[5:5 + Lb, :] * cw_ref[0:1, :]
    conv = conv + xpad[6:6 + Lb, :] * cw_ref[1:2, :]
    conv = conv + xpad[7:7 + Lb, :] * cw_ref[2:3, :]
    conv = conv + xpad[8:8 + Lb, :] * cw_ref[3:4, :]
    xc_s[...] = _silu(conv)

    a_neg = -jnp.exp(alog_ref[...])

    def iota(shape, d):
        return lax.broadcasted_iota(I32, shape, d)

    lq = Q.bit_length() - 1
    e_p = (iota((H, H * P), 1) >> 6 == iota((H, H * P), 0)).astype(F32)
    e_q = (iota((H, HQ), 1) >> lq == iota((H, HQ), 0)).astype(F32)
    j_of = iota((Q, HQ), 1) & (Q - 1)
    t_of = iota((Q, HQ), 0)
    u_rep = (t_of <= j_of).astype(F32)
    causal = j_of <= t_of
    c2 = iota((Q, 2 * Q), 1)
    r2 = iota((Q, 2 * Q), 0)
    seg_lhs = jnp.where(c2 < Q, (c2 <= r2).astype(F32), -1.0)
    tril = (iota((Q, Q), 1) <= iota((Q, Q), 0)).astype(F32)
    ones_qn = jnp.ones((Q, N), F32)
    bd_mask = (iota((hp * Q, hp * P), 0) >> lq) == (iota((hp * Q, hp * P), 1) >> 6)

    def chunk(c, carry):
        r0 = pl.multiple_of(c * Q, Q)
        dt = _softplus(sm_ref[pl.ds(r0, Q), SM_DT:SM_DT + H] + dtb_ref[...])
        a = dt * a_neg
        both = _dot(jnp.concatenate([dt, a], axis=0), e_p, precision=HIGHEST)
        dt_full, a_full = both[:Q], both[Q:]
        acum_full = _dot(tril, a_full, precision=HIGHEST)
        tot_full = acum_full[Q - 1:Q, :]
        xs = xc_s[pl.ds(r0, Q), 0:D_SSM]
        bm = xc_s[pl.ds(r0, Q), D_SSM:D_SSM + N_SSD_GROUPS * N].astype(BF16)
        cm = xc_s[pl.ds(r0, Q), D_SSM + N_SSD_GROUPS * N:CONV_DIM].astype(BF16)
        xd = xs * dt_full
        xdw = (xd * jnp.exp(tot_full - acum_full)).astype(BF16)
        xdb = xd.astype(BF16)

        a_q = _dot(a, e_q, precision=HIGHEST)
        seg = _dot(seg_lhs, jnp.concatenate([a_q, a_q * u_rep], axis=0), precision=HIGHEST)
        decay = jnp.exp(jnp.where(causal, seg, -jnp.inf))
        cb_parts = []
        for g in range(N_SSD_GROUPS):
            bm_g = bm[:, g * N:(g + 1) * N]
            rep = jnp.concatenate([bm_g] * (H // N_SSD_GROUPS), axis=0)
            cb_parts.append(_dot(cm[:, g * N:(g + 1) * N], rep, _NT))
        mmat = (jnp.concatenate(cb_parts, axis=1) * decay).astype(BF16)

        y_parts = []
        for d in range(n_diag):
            xd_d = xdb[:, d * hp * P:(d + 1) * hp * P]
            bd = jnp.where(bd_mask, jnp.concatenate([xd_d] * hp, axis=0), jnp.zeros((), BF16))
            y_parts.append(_dot(mmat[:, d * hp * Q:(d + 1) * hp * Q], bd))
        y_diag = jnp.concatenate(y_parts, axis=1) if n_diag > 1 else y_parts[0]

        totcol = _dot(a_full, ones_qn, _TN, precision=HIGHEST)
        y_off_parts = []
        for g in range(N_SSD_GROUPS):
            st_g = state[g * GP:(g + 1) * GP, :]
            y_off_parts.append(_dot(cm[:, g * N:(g + 1) * N], st_g.astype(BF16), _NT))
            upd = _dot(xdw[:, g * GP:(g + 1) * GP], bm[:, g * N:(g + 1) * N], _TN)
            state[g * GP:(g + 1) * GP, :] = jnp.exp(totcol[g * GP:(g + 1) * GP, :]) * st_g + upd
        y_off = jnp.concatenate(y_off_parts, axis=1) * jnp.exp(acum_full)

        y = (y_diag + y_off) + dskip_ref[...] * xs
        gt = y * _silu(zs_ref[pl.ds(r0, Q), :])
        for g in range(N_SSD_GROUPS):
            gg = gt[:, g * GP:(g + 1) * GP]
            ms = jnp.mean(gg * gg, axis=-1, keepdims=True)
            y_ref[pl.ds(r0, Q), g * GP:(g + 1) * GP] = (
                gg * lax.rsqrt(ms + RMS_EPS) * nw_ref[:, g * GP:(g + 1) * GP])
        return carry

    lax.fori_loop(0, Lb // Q, chunk, 0)

    @pl.when(i == nblk - 1)
    def _():
        snew_ref[0] = state[...]


def _ssd(h2d, conv_prev8, ssm_prev, conv_w, conv_b, dt_bias, a_log, dskip_full, norm_w, *, B, L, Q, Lb):
    assert L % Lb == 0 and Lb % Q == 0 and Lb % 8 == 0
    nblk = L // Lb
    HP = N_SSD_HEADS * SSD_HEAD_DIM
    row = lambda b, i: b * nblk + i
    full2 = lambda shp: pl.BlockSpec(shp, lambda b, i: (0, 0))
    return pl.pallas_call(
        functools.partial(_ssd_kernel, Q=Q, Lb=Lb),
        grid=(B, nblk),
        in_specs=[
            pl.BlockSpec((Lb, D_SSM), lambda b, i: (row(b, i), OFF_ZS // D_SSM)),
            pl.BlockSpec((Lb, CONV_DIM), lambda b, i: (row(b, i), OFF_XBC // CONV_DIM)),
            pl.BlockSpec((Lb, 128), lambda b, i: (row(b, i), OFF_SMALL // 128)),
            pl.BlockSpec((1, 8, CONV_DIM), lambda b, i: (b, 0, 0)),
            pl.BlockSpec((1, HP, D_STATE), lambda b, i: (b, 0, 0)),
            full2((CONV_W, CONV_DIM)),
            full2((1, CONV_DIM)),
            full2((1, N_SSD_HEADS)),
            full2((1, N_SSD_HEADS)),
            full2((1, D_SSM)),
            full2((1, D_SSM)),
        ],
        out_specs=[
            pl.BlockSpec((Lb, D_SSM), lambda b, i: (row(b, i), 0)),
            pl.BlockSpec((1, HP, D_STATE), lambda b, i: (b, 0, 0)),
        ],
        out_shape=[
            jax.ShapeDtypeStruct((B * L, D_SSM), F32),
            jax.ShapeDtypeStruct((B, HP, D_STATE), F32),
        ],
        scratch_shapes=[
            pltpu.VMEM((Lb + 8, CONV_DIM), F32),
            pltpu.VMEM((Lb, CONV_DIM), F32),
            pltpu.VMEM((HP, D_STATE), F32),
        ],
        compiler_params=pltpu.CompilerParams(
            dimension_semantics=("arbitrary", "arbitrary"), vmem_limit_bytes=VMEM_LIMIT),
        name="ssd",
    )(h2d, h2d, h2d, conv_prev8, ssm_prev, conv_w, conv_b, dt_bias, a_log, dskip_full, norm_w)


def _alibi_slope(h):
    return float(2.0 ** (-8.0 * (h + 1) / N_ATT_HEADS))


LOG2E = float(np.log2(np.e))
POS_SPLIT = 128
N_POS_ROWS = 6


def _alibi_rows():
    rows = np.zeros((N_ATT_HEADS, ATT_HEAD_DIM, 1), np.float32)
    for h in range(N_ATT_HEADS):
        pieces = _split_bf16x3(np.float32(_alibi_slope(h) * LOG2E))
        rows[h, 0:3, 0] = [p * POS_SPLIT for p in pieces]
        rows[h, 3:6, 0] = pieces
    return rows


def _dsa_prompt_kernel(q_ref, qi_ref, sm_ref, arow_ref, kb_ref, vt_ref, kib_ref, o_ref,
                       sc_s, lg_s, p_s, qa_s, m_s, a_s, l_s, acc_s, *, Qb, k_sel):
    Kb = Qb
    i = pl.program_id(0)
    nk = i + 1
    rpg = N_ATT_HEADS // N_KV_HEADS
    D = ATT_HEAD_DIM

    q_t = q_ref[...].T * (D ** -0.5 * LOG2E)
    for h in range(N_ATT_HEADS):
        qa_s[h, 0:D, :] = q_t[h * D:(h + 1) * D, :].astype(BF16)
        qa_s[h, D:2 * D, :] = jnp.broadcast_to(arow_ref[h], (D, Qb)).astype(BF16)
    qi_t = qi_ref[...].T.astype(BF16)
    w_t = sm_ref[...].T[SM_WI:SM_WI + IDX_HEADS, :] * (IDX_DIM ** -0.5 * IDX_HEADS ** -0.5)
    qpos = i * Qb + lax.broadcasted_iota(I32, (1, Qb), 1)
    chunk_end = (((qpos >> 6) + 1) << 6) - 1
    krow = lax.broadcasted_iota(I32, (Kb, 1), 0)

    def score_blk(j, carry):
        kib = kib_ref[j]
        score = jnp.zeros((Kb, Qb), F32)
        for h in range(IDX_HEADS):
            s = _dot(kib, qi_t[h * IDX_DIM:(h + 1) * IDX_DIM, :])
            score = score + w_t[h:h + 1, :] * jnp.maximum(s, 0.0)
        sc_s[j] = jnp.where((j * Kb + krow) <= chunk_end, score, -jnp.inf)
        return carry

    lax.fori_loop(0, nk, score_blk, 0)

    def count(pred):
        def body(j, acc):
            hit = jnp.where(pred(sc_s[j]), 1, 0).astype(I32)
            return acc + hit.reshape(Kb // 8, 8, Qb).sum(axis=0)
        acc = lax.fori_loop(0, nk, body, jnp.zeros((8, Qb), I32))
        return acc.sum(axis=0, keepdims=True)

    thr, n_ge, few = _kth_largest(count, k_sel, (1, Qb))
    has_ties = jnp.max(jnp.where(n_ge > k_sel, 1, 0)) > 0
    need = lax.cond(
        has_ties,
        lambda: jnp.where(few, jnp.float32(3e38), (k_sel - count(lambda s: s > thr)).astype(F32)),
        lambda: jnp.zeros((1, Qb), F32))

    m_s[...] = jnp.full(m_s.shape, NEG_BIG, F32)
    l_s[...] = jnp.zeros(l_s.shape, F32)
    acc_s[...] = jnp.zeros(acc_s.shape, F32)

    def selected(sc, seen):
        def no_ties():
            return jnp.where(sc >= thr, 1.0, 0.0), seen

        def with_ties():
            tril =(lax.broadcasted_iota(I32, (Kb, Kb), 1) <= lax.broadcasted_iota(I32, (Kb, Kb), 0))
            eq = sc == thr
            rank = seen + _dot(tril.astype(BF16), jnp.where(eq, 1.0, 0.0).astype(BF16))
            return jnp.where((sc > thr) | (eq & (rank <= need)), 1.0, 0.0), rank[Kb - 1:Kb, :]

        return lax.cond(has_ties, with_ties, no_ties)

    def attend(j, seen, diagonal):
        sel_f, seen = selected(sc_s[j], seen)
        sel = sel_f > 0.5
        kblk = kb_ref[j]
        vblk = vt_ref[j]
        kpos = j * Kb + krow
        if diagonal:
            adj = jnp.minimum(kpos, 2 * qpos - kpos).astype(F32)
        else:
            lane = lax.broadcasted_iota(I32, (Kb, D), 1)
            hi = (kpos >> 7).astype(F32)
            lo = (kpos & (POS_SPLIT - 1)).astype(F32)
            feat = jnp.where(lane < 3, hi, jnp.where(lane < N_POS_ROWS, lo, 0.0)).astype(BF16)
            kcat = [jnp.concatenate([kblk[:, g * D:(g + 1) * D], feat], axis=1) for g in range(N_KV_HEADS)]
        for h in range(N_ATT_HEADS):
            g = h // rpg
            if diagonal:
                lg = _dot(kblk[:, g * D:(g + 1) * D], qa_s[h, 0:D, :]) + (_alibi_slope(h) * LOG2E) * adj
            else:
                lg = _dot(kcat[g], qa_s[h])
            lg = jnp.where(sel, lg, -jnp.inf)
            lg_s[h] = lg
            m_old = m_s[h:h + 1, :]
            m_new = jnp.maximum(m_old, jnp.max(lg, axis=0, keepdims=True))
            a_s[h:h + 1, :] = jnp.exp2(m_old - m_new)
            m_s[h:h + 1, :] = m_new
        for h in range(N_ATT_HEADS):
            p = jnp.exp2(lg_s[h] - m_s[h:h + 1, :])
            l_s[h:h + 1, :] = a_s[h:h + 1, :] * l_s[h:h + 1, :] + jnp.sum(p, axis=0, keepdims=True)
            p_s[h] = p.astype(BF16)
        for h in range(N_ATT_HEADS):
            g = h // rpg
            acc_s[h] = a_s[h:h + 1, :] * acc_s[h] + _dot(vblk[g * D:(g + 1) * D, :], p_s[h])
        return seen

    seen = lax.fori_loop(0, i, lambda j, s: attend(j, s, False), jnp.zeros((1, Qb), F32))
    attend(i, seen, True)

    for h in range(N_ATT_HEADS):
        out_t = acc_s[h] * (1.0 / l_s[h:h + 1, :])
        o_ref[:, h * D:(h + 1) * D] = out_t.T


def _dsa_prompt(h2d, kb3, vt3, kib3, *, S, Qb, k_sel):
    nkb = S // Qb
    const3 = lambda shp: pl.BlockSpec(shp, lambda i: (0, 0, 0), pipeline_mode=pl.Buffered(1))
    return pl.pallas_call(
        functools.partial(_dsa_prompt_kernel, Qb=Qb, k_sel=k_sel),
        grid=(nkb,),
        in_specs=[
            pl.BlockSpec((Qb, D_ATT), lambda i: (i, OFF_Q // D_ATT)),
            pl.BlockSpec((Qb, IDX_HEADS * IDX_DIM), lambda i: (i, OFF_QI // (IDX_HEADS * IDX_DIM))),
            pl.BlockSpec((Qb, 128), lambda i: (i, OFF_SMALL // 128)),
            const3((N_ATT_HEADS, ATT_HEAD_DIM, 1)),
            const3((nkb, Qb, KV_DIM)),
            const3((nkb, KV_DIM, Qb)),
            const3((nkb, Qb, IDX_DIM)),
        ],
        out_specs=pl.BlockSpec((Qb, D_ATT), lambda i: (i, 0)),
        out_shape=jax.ShapeDtypeStruct((S, D_ATT), F32),
        scratch_shapes=[
            pltpu.VMEM((nkb, Qb, Qb), F32),
            pltpu.VMEM((N_ATT_HEADS, Qb, Qb), F32),
            pltpu.VMEM((N_ATT_HEADS, Qb, Qb), BF16),
            pltpu.VMEM((N_ATT_HEADS, 2 * ATT_HEAD_DIM, Qb), BF16),
            pltpu.VMEM((N_ATT_HEADS, Qb), F32),
            pltpu.VMEM((N_ATT_HEADS, Qb), F32),
            pltpu.VMEM((N_ATT_HEADS, Qb), F32),
            pltpu.VMEM((N_ATT_HEADS, ATT_HEAD_DIM, Qb), F32),
        ],
        compiler_params=pltpu.CompilerParams(
            dimension_semantics=("arbitrary",), vmem_limit_bytes=VMEM_LIMIT),
        name="dsa_prompt",
    )(h2d, h2d, h2d, jnp.asarray(_alibi_rows()), kb3, vt3, kib3)


def _dsa_sample_kernel(qr_ref, qir_ref, wr_ref, slope_ref, ck_ref, cv_ref, cki_ref, kn_ref, vn_ref, kin_ref,
                       o_ref, keys_s, sel_s, *, T, P, k_sel):
    LN = 128
    L = P + LN
    rpg = N_ATT_HEADS // N_KV_HEADS
    R = rpg * T

    qi_b = qir_ref[0].astype(BF16)
    w_col = wr_ref[0] * (IDX_DIM ** -0.5 * IDX_HEADS ** -0.5)

    def head_sum(s):
        return (jnp.maximum(s, 0.0) * w_col).reshape(IDX_HEADS, T, s.shape[-1]).sum(axis=0)

    sc_c = head_sum(_dot(qi_b, cki_ref[0].astype(BF16), _NT))
    sc_n = head_sum(_dot(qi_b, kin_ref[0].astype(BF16), _NT))
    keys_s[:, 0:P] = sc_c
    col_n = lax.broadcasted_iota(I32, (T, LN), 1)
    keys_s[:, P:L] = jnp.where(col_n < T, sc_n, -jnp.inf)

    def count(pred):
        return jnp.sum(jnp.where(pred(keys_s[...]), 1, 0).astype(I32), axis=1, keepdims=True)

    thr, _, few = _kth_largest(count, k_sel, (T, 1))
    need = jnp.where(few, jnp.float32(3e38), (k_sel - count(lambda s: s > thr)).astype(F32))

    triu = (lax.broadcasted_iota(I32, (LN, LN), 0) <= lax.broadcasted_iota(I32, (LN, LN), 1)).astype(BF16)
    seen = jnp.zeros((T, 1), F32)
    for jb in range(L // LN):
        kk = keys_s[:, jb * LN:(jb + 1) * LN]
        eq = kk == thr
        rank = seen + _dot(jnp.where(eq, 1.0, 0.0).astype(BF16), triu)
        sel_s[:, jb * LN:(jb + 1) * LN] = jnp.where((kk > thr) | (eq & (rank <= need)), 1.0, 0.0)
        seen = rank[:, LN - 1:LN]

    qpos = P + lax.broadcasted_iota(I32, (T, L), 0)
    kpos = lax.broadcasted_iota(I32, (T, L), 1)
    dist = jnp.abs(qpos - kpos).astype(F32)
    dist_r = jnp.concatenate([dist] * rpg, axis=0)
    sel_r = jnp.concatenate([sel_s[...]] * rpg, axis=0) > 0.5
    for g in range(N_KV_HEADS):
        sl = slice(g * ATT_HEAD_DIM, (g + 1) * ATT_HEAD_DIM)
        qg = (qr_ref[0, g * R:(g + 1) * R, :] * (ATT_HEAD_DIM ** -0.5)).astype(BF16)
        cache_rows = pl.ds(g, P, stride=N_KV_HEADS)
        lg = jnp.concatenate([_dot(qg, ck_ref[0, cache_rows, :].astype(BF16), _NT),
                              _dot(qg, kn_ref[0, :, sl].astype(BF16), _NT)], axis=1)
        lg = jnp.where(sel_r, lg - slope_ref[g * R:(g + 1) * R, :] * dist_r, -jnp.inf)
        m = jnp.max(lg, axis=1, keepdims=True)
        p = jnp.exp(lg - m)
        den = jnp.sum(p, axis=1, keepdims=True)
        pb = p.astype(BF16)
        out = (_dot(pb[:, 0:P], cv_ref[0, cache_rows, :].astype(BF16))
               + _dot(pb[:, P:L], vn_ref[0, :, sl].astype(BF16)))
        out = out * (1.0 / den)
        for r in range(rpg):
            hh = g * rpg + r
            o_ref[0, :, hh * ATT_HEAD_DIM:(hh + 1) * ATT_HEAD_DIM] = out[r * T:(r + 1) * T, :]


def _dsa_sample(q_rows, qi_rows, w_rows, slope_rows, ck, cv, cki, kn, vn, kin, *, Bd, T, P, k_sel):
    LN = 128
    per_b = lambda shp: pl.BlockSpec(shp, lambda b: (b, 0, 0))
    return pl.pallas_call(
        functools.partial(_dsa_sample_kernel, T=T, P=P, k_sel=k_sel),
        grid=(Bd,),
        in_specs=[
            per_b((1, N_ATT_HEADS * T, ATT_HEAD_DIM)),
            per_b((1, IDX_HEADS * T, IDX_DIM)),
            per_b((1, IDX_HEADS * T, 1)),
            pl.BlockSpec((N_ATT_HEADS * T, 1), lambda b: (0, 0)),
            per_b((1, N_KV_HEADS * P, ATT_HEAD_DIM)),
            per_b((1, N_KV_HEADS * P, ATT_HEAD_DIM)),
            per_b((1, P, IDX_DIM)),
            per_b((1, LN, KV_DIM)),
            per_b((1, LN, KV_DIM)),
            per_b((1, LN, IDX_DIM)),
        ],
        out_specs=per_b((1, T, D_ATT)),
        out_shape=jax.ShapeDtypeStruct((Bd, T, D_ATT), F32),
        scratch_shapes=[
            pltpu.VMEM((T, P + LN), F32),
            pltpu.VMEM((T, P + LN), F32),
        ],
        compiler_params=pltpu.CompilerParams(
            dimension_semantics=("arbitrary",), vmem_limit_bytes=VMEM_LIMIT),
        name="dsa_sample",
    )(q_rows, qi_rows, w_rows, slope_rows, ck, cv, cki, kn, vn, kin)


def _merge_kernel(x_ref, ys_ref, att_ref, za_ref, wo_ref, g_ref, b_ref, o_ref):
    att = att_ref[...] * _silu(za_ref[...])
    mix = _dot(ys_ref[...].astype(BF16), wo_ref[0:D_SSM, :]) + _dot(att.astype(BF16), wo_ref[D_SSM:, :])
    hres = ALPHA * x_ref[...] + mix
    mu = jnp.mean(hres, axis=-1, keepdims=True)
    hc = hres - mu
    var = jnp.mean(hc * hc, axis=-1, keepdims=True)
    o_ref[...] = hc * lax.rsqrt(var + LN_EPS) * g_ref[...] + b_ref[...]


def _merge(x2d, y_ssd, att, h2d, w_out_b, ln_g, ln_b, tm):
    m = x2d.shape[0]
    rows = lambda i: (i, 0)
    return pl.pallas_call(
        _merge_kernel,
        grid=(m // tm,),
        in_specs=[
            pl.BlockSpec((tm, D_MODEL), rows),
            pl.BlockSpec((tm, D_SSM), rows),
            pl.BlockSpec((tm, D_ATT), rows),
            pl.BlockSpec((tm, D_ATT), lambda i: (i, OFF_ZA // D_ATT)),
            pl.BlockSpec((D_SSM + D_ATT, D_MODEL), lambda i: (0, 0), pipeline_mode=pl.Buffered(1)),
            pl.BlockSpec((1, D_MODEL), lambda i: (0, 0)),
            pl.BlockSpec((1, D_MODEL), lambda i: (0, 0)),
        ],
        out_specs=pl.BlockSpec((tm, D_MODEL), rows),
        out_shape=jax.ShapeDtypeStruct((m, D_MODEL), F32),
        compiler_params=pltpu.CompilerParams(
            dimension_semantics=("arbitrary",), vmem_limit_bytes=VMEM_LIMIT),
        name="merge",
    )(x2d, y_ssd, att, h2d, w_out_b, ln_g, ln_b)


def _regroup_w_in(w_in):
    s = _SRC
    cols = [w_in[:, s["zs"]:s["zs"] + 1024], w_in[:, s["q"]:s["q"] + 1024], w_in[:, s["za"]:s["za"] + 1024],
            w_in[:, s["xbc"]:s["xbc"] + CONV_DIM], w_in[:, s["qi"]:s["qi"] + 512],
            w_in[:, s["k"]:s["k"] + KV_DIM], w_in[:, s["v"]:s["v"] + KV_DIM],
            w_in[:, s["dt"]:s["dt"] + 16], w_in[:, s["ki"]:s["ki"] + IDX_DIM], w_in[:, s["wi"]:s["wi"] + IDX_HEADS],
            jnp.zeros((D_MODEL, 128 - 16 - IDX_DIM - IDX_HEADS), w_in.dtype)]
    return jnp.concatenate(cols, axis=1).astype(BF16)


def _row_tile(m):
    return 256 if m % 256 == 0 else m


def kernel(x_prompt, x_sample, cache_k, cache_v, cache_kidx, state_ssm, state_conv, w_in, conv_w, conv_b,
           dt_bias, a_log, d_skip, ssd_norm_w, w_out, ln_g, ln_b):
    B, S, _ = x_prompt.shape
    Bd, T, _ = x_sample.shape
    P = cache_k.shape[1]
    assert B == 1 and S % 256 == 0 and T % 8 == 0 and T >= CONV_W - 1 and T <= 128 and P % 128 == 0
    HP = N_SSD_HEADS * SSD_HEAD_DIM

    w_pad = _regroup_w_in(w_in)
    w_out_b = w_out.astype(BF16)
    conv_b2 = conv_b.reshape(1, CONV_DIM)
    dtb2 = dt_bias.reshape(1, N_SSD_HEADS)
    alog2 = a_log.reshape(1, N_SSD_HEADS)
    dskip_full = jnp.repeat(d_skip, SSD_HEAD_DIM).reshape(1, D_SSM)
    nw2 = ssd_norm_w.reshape(1, D_SSM)
    g2 = ln_g.reshape(1, D_MODEL)
    b2 = ln_b.reshape(1, D_MODEL)

    xp = x_prompt.reshape(S, D_MODEL)
    h_p, k_p, v_p, ki_p = _project(xp, w_pad, _row_tile(S))
    conv0 = jnp.zeros((1, 8, CONV_DIM), F32)
    ssm0 = jnp.zeros((1, HP, D_STATE), F32)
    yssd_p, ssm_p = _ssd(h_p, conv0, ssm0, conv_w, conv_b2, dtb2, alog2, dskip_full, nw2,
                         B=1, L=S, Q=CHUNK, Lb=256)
    Qb = 256
    nkb = S // Qb
    kb3 = k_p.astype(BF16).reshape(nkb, Qb, KV_DIM)
    vt3 = v_p.astype(BF16).reshape(nkb, Qb, KV_DIM).transpose(0, 2, 1)
    kib3 = ki_p.astype(BF16).reshape(nkb, Qb, IDX_DIM)
    att_p = _dsa_prompt(h_p, kb3, vt3, kib3, S=S, Qb=Qb, k_sel=min(TOPK_MAX, S // 4))
    y_p = _merge(xp, yssd_p, att_p, h_p, w_out_b, g2, b2, _row_tile(S))

    M = Bd * T
    xs = x_sample.reshape(M, D_MODEL)
    h_s, k_s, v_s, ki_s = _project(xs, w_pad, _row_tile(M))
    conv_prev8 = jnp.pad(state_conv, ((0, 0), (8 - (CONV_W - 1), 0), (0, 0)))
    yssd_s, ssm_s = _ssd(h_s, conv_prev8, state_ssm.reshape(Bd, HP, D_STATE), conv_w, conv_b2, dtb2, alog2,
                         dskip_full, nw2, B=Bd, L=T, Q=T, Lb=T)
    h_s3 = h_s.reshape(Bd, T, D_PAD)

    def head_rows(a, nh, dh):
        return a.reshape(Bd, T, nh, dh).transpose(0, 2, 1, 3).reshape(Bd, nh * T, dh)

    q_rows = head_rows(h_s3[:, :, OFF_Q:OFF_Q + D_ATT], N_ATT_HEADS, ATT_HEAD_DIM)
    qi_rows = head_rows(h_s3[:, :, OFF_QI:OFF_QI + IDX_HEADS * IDX_DIM], IDX_HEADS, IDX_DIM)
    w_rows = head_rows(h_s3[:, :, OFF_SMALL + SM_WI:OFF_SMALL + SM_WI + IDX_HEADS], IDX_HEADS, 1)
    slope_rows = jnp.asarray(np.repeat([_alibi_slope(h) for h in range(N_ATT_HEADS)], T).reshape(-1, 1), F32)
    padn = lambda a: jnp.pad(a.reshape(Bd, T, -1), ((0, 0), (0, 128 - T), (0, 0)))
    att_s = _dsa_sample(q_rows, qi_rows, w_rows, slope_rows,
                        cache_k.reshape(Bd, N_KV_HEADS * P, ATT_HEAD_DIM),
                        cache_v.reshape(Bd, N_KV_HEADS * P, ATT_HEAD_DIM), cache_kidx,
                        padn(k_s), padn(v_s), padn(ki_s), Bd=Bd, T=T, P=P, k_sel=min(TOPK_MAX, (P + T) // 4))
    y_s = _merge(xs, yssd_s, att_s.reshape(M, D_ATT), h_s, w_out_b, g2, b2, _row_tile(M))

    kv4 = lambda a, b_, l: a.reshape(b_, l, N_KV_HEADS, ATT_HEAD_DIM)
    st4 = lambda a, b_: a.reshape(b_, N_SSD_HEADS, SSD_HEAD_DIM, D_STATE)
    conv_p = h_p[S - (CONV_W - 1):, OFF_XBC:OFF_XBC + CONV_DIM].reshape(1, CONV_W - 1, CONV_DIM)
    conv_s = h_s3[:, T - (CONV_W - 1):, OFF_XBC:OFF_XBC + CONV_DIM]
    return (y_p.reshape(1, S, D_MODEL), y_s.reshape(Bd, T, D_MODEL),
            kv4(k_p, 1, S), kv4(v_p, 1, S), ki_p.reshape(1, S, IDX_DIM), st4(ssm_p, 1), conv_p,
            kv4(k_s, Bd, T), kv4(v_s, Bd, T), ki_s.reshape(Bd, T, IDX_DIM), st4(ssm_s, Bd), conv_s)
```

```python
import functools

import numpy as np
import jax
import jax.numpy as jnp
from jax import lax
from jax.experimental import pallas as pl
from jax.experimental.pallas import tpu as pltpu

F32 = jnp.float32
BF16 = jnp.bfloat16
I32 = jnp.int32
HIGHEST = lax.Precision.HIGHEST

D_MODEL = 1024
CHUNK = 64
D_SSM = 1024
SSD_HEAD_DIM = 64
N_SSD_HEADS = 16
N_SSD_GROUPS = 2
D_STATE = 128
CONV_W = 4
CONV_DIM = D_SSM + 2 * N_SSD_GROUPS * D_STATE
D_ATT = 1024
ATT_HEAD_DIM = 128
N_ATT_HEADS = 8
N_KV_HEADS = 2
KV_DIM = N_KV_HEADS * ATT_HEAD_DIM
IDX_HEADS = 8
IDX_DIM = 64
TOPK_MAX = 256
ALPHA = 2.0 ** 0.25
LN_EPS = 1e-5
RMS_EPS = 1e-5

_SRC = dict(zs=0, xbc=1024, dt=2560, q=2576, k=3600, v=3856, za=4112, qi=5136, ki=5648, wi=5712)
OFF_ZS, OFF_Q, OFF_ZA, OFF_XBC, OFF_QI, OFF_K, OFF_V, OFF_SMALL = 0, 1024, 2048, 3072, 4608, 5120, 5376, 5632
D_PAD = 5760
SM_DT, SM_KI, SM_WI = 0, 16, 80

INT_MIN = -(2 ** 31)
NEG_BIG = -1e30

VMEM_LIMIT = 56 * 1024 * 1024


def _dot(a, b, dims=(((1,), (0,)), ((), ())), precision=None):
    return lax.dot_general(a, b, dims, precision=precision, preferred_element_type=F32)


_NT = (((1,), (1,)), ((), ()))
_TN = (((0,), (0,)), ((), ()))


def _silu(x):
    return x * (1.0 / (1.0 + jnp.exp(-x)))


def _softplus(x):
    return jnp.maximum(x, 0.0) + jnp.log1p(jnp.exp(-jnp.abs(x)))


def _pattern_to_float(u):
    key = u ^ INT_MIN
    return pltpu.bitcast(key ^ ((key >> 31) & jnp.int32(0x7FFFFFFF)), F32)


def _kth_largest(count, k_sel, shape):
    def bit_step(t, carry):
        prefix, n_ge = carry
        cand_u = prefix | lax.shift_left(jnp.int32(1), 31 - t)
        cnt = count(lambda s: s >= _pattern_to_float(cand_u))
        ok = cnt >= k_sel
        return jnp.where(ok, cand_u, prefix), jnp.where(ok, cnt, n_ge)

    n_valid = count(lambda s: s > -jnp.inf)
    prefix, n_ge = lax.fori_loop(0, 32, bit_step, (jnp.zeros(shape, I32), n_valid))
    few = n_valid < k_sel
    thr = jnp.where(few, jnp.float32(np.finfo(np.float32).min), _pattern_to_float(prefix))
    return thr, jnp.where(few, k_sel, n_ge), few


def _split_bf16x3(x):
    parts, r = [], np.float32(x)
    for _ in range(3):
        p = np.float32(np.asarray(r, dtype=BF16))
        parts.append(float(p))
        r = np.float32(r - p)
    assert r == 0.0
    return parts


_PROJ_CHUNKS = tuple((c, 512) for c in range(0, 5632, 512)) + ((5632, 128),)


def _proj_kernel(x_ref, w_ref, h_ref, k_ref, v_ref, ki_ref, kb_ref, vt_ref, kib_ref):
    xb = x_ref[...].astype(BF16)
    for c0, cw in _PROJ_CHUNKS:
        h_ref[:, c0:c0 + cw] = _dot(xb, w_ref[:, c0:c0 + cw])
    k = h_ref[:, OFF_K:OFF_K + KV_DIM]
    v = h_ref[:, OFF_V:OFF_V + KV_DIM]
    ki = h_ref[:, OFF_SMALL + SM_KI:OFF_SMALL + SM_KI + IDX_DIM]
    k_ref[...] = k
    v_ref[...] = v
    ki_ref[...] = ki
    kb_ref[0] = k.astype(BF16)
    kib_ref[0] = ki.astype(BF16)
    v_t = v.T
    for g in range(N_KV_HEADS):
        r0 = g * V_AUG_ROWS
        vt_ref[0, r0:r0 + ATT_HEAD_DIM, :] = v_t[g * ATT_HEAD_DIM:(g + 1) * ATT_HEAD_DIM, :].astype(BF16)
        vt_ref[0, r0 + ATT_HEAD_DIM:r0 + V_AUG_ROWS, :] = jnp.ones(
            (V_AUG_ROWS - ATT_HEAD_DIM, v_t.shape[1]), BF16)


def _project(x2d, w_pad, tm):
    m = x2d.shape[0]
    assert m % tm == 0
    return pl.pallas_call(
        _proj_kernel,
        grid=(m // tm,),
        in_specs=[
            pl.BlockSpec((tm, D_MODEL), lambda i: (i, 0)),
            pl.BlockSpec((D_MODEL, D_PAD), lambda i: (0, 0), pipeline_mode=pl.Buffered(1)),
        ],
        out_specs=[
            pl.BlockSpec((tm, D_PAD), lambda i: (i, 0)),
            pl.BlockSpec((tm, KV_DIM), lambda i: (i, 0)),
            pl.BlockSpec((tm, KV_DIM), lambda i: (i, 0)),
            pl.BlockSpec((tm, IDX_DIM), lambda i: (i, 0)),
            pl.BlockSpec((1, tm, KV_DIM), lambda i: (i, 0, 0)),
            pl.BlockSpec((1, N_KV_HEADS * V_AUG_ROWS, tm), lambda i: (i, 0, 0)),
            pl.BlockSpec((1, tm, IDX_DIM), lambda i: (i, 0, 0)),
        ],
        out_shape=[
            jax.ShapeDtypeStruct((m, D_PAD), F32),
            jax.ShapeDtypeStruct((m, KV_DIM), F32),
            jax.ShapeDtypeStruct((m, KV_DIM), F32),
            jax.ShapeDtypeStruct((m, IDX_DIM), F32),
            jax.ShapeDtypeStruct((m // tm, tm, KV_DIM), BF16),
            jax.ShapeDtypeStruct((m // tm, N_KV_HEADS * V_AUG_ROWS, tm), BF16),
            jax.ShapeDtypeStruct((m // tm, tm, IDX_DIM), BF16),
        ],
        compiler_params=pltpu.CompilerParams(
            dimension_semantics=("arbitrary",), vmem_limit_bytes=VMEM_LIMIT),
        name="proj",
    )(x2d, w_pad)


def _ssd_kernel(zs_ref, xbc_ref, sm_ref, cprev_ref, sprev_ref, cw_ref, cb_ref, dtb_ref, alog_ref,
                dskip_ref, nw_ref, y_ref, snew_ref, xpad, xc_s, state, *, Q, Lb):
    i = pl.program_id(1)
    nblk = pl.num_programs(1)
    H, P, N = N_SSD_HEADS, SSD_HEAD_DIM, D_STATE
    HQ = H * Q
    hp = min(H, 256 // Q)
    n_diag = H // hp
    GP = (H // N_SSD_GROUPS) * P

    @pl.when(i == 0)
    def _():
        xpad[0:8, :] = cprev_ref[0]
        state[...] = sprev_ref[0].T

    @pl.when(i > 0)
    def _():
        xpad[0:8, :] = xpad[Lb:Lb + 8, :]

    xpad[8:8 + Lb, :] = xbc_ref[...]
    conv = cb_ref[...] + xpad[5:5 + Lb, :] * cw_ref[0:1, :]
    conv = conv + xpad[6:6 + Lb, :] * cw_ref[1:2, :]
    conv = conv + xpad[7:7 + Lb, :] * cw_ref[2:3, :]
    conv = conv + xpad[8:8 + Lb, :] * cw_ref[3:4, :]
    xc_s[...] = _silu(conv)

    a_neg = -jnp.exp(alog_ref[...])

    def iota(shape, d):
        return lax.broadcasted_iota(I32, shape, d)

    lq = Q.bit_length() - 1
    e_p = (iota((H, H * P), 1) >> 6 == iota((H, H * P), 0)).astype(BF16)
    e_q = (iota((H, HQ), 1) >> lq == iota((H, HQ), 0)).astype(BF16)
    j_of = iota((Q, HQ), 1) & (Q - 1)
    t_of = iota((Q, HQ), 0)
    u_rep = (t_of <= j_of).astype(F32)
    causal = j_of <= t_of
    c6 = iota((Q, 6 * Q), 1) & (2 * Q - 1)
    r6 = iota((Q, 6 * Q), 0)
    seg_lhs3 = jnp.where(c6 < Q, (c6 <= r6).astype(F32), -1.0).astype(BF16)
    tril3 = ((iota((Q, 3 * Q), 1) & (Q - 1)) <= iota((Q, 3 * Q), 0)).astype(BF16)
    bd_mask = (iota((hp * Q, hp * P), 0) >> lq) == (iota((hp * Q, hp * P), 1) >> 6)

    def split3(x):
        x1 = x.astype(BF16)
        r1 = x - x1.astype(F32)
        x2 = r1.astype(BF16)
        return x1, x2, (r1 - x2.astype(F32)).astype(BF16)

    def expand(x, e):
        rows = x.shape[0]
        r = _dot(jnp.concatenate(split3(x), axis=0), e)
        return (r[0:rows] + r[rows:2 * rows]) + r[2 * rows:3 * rows]

    def chunk(c, carry):
        r0 = pl.multiple_of(c * Q, Q)
        dt = _softplus(sm_ref[pl.ds(r0, Q), SM_DT:SM_DT + H] + dtb_ref[...])
        a = dt * a_neg
        both = expand(jnp.concatenate([dt, a], axis=0), e_p)
        dt_full, a_full = both[:Q], both[Q:]
        acum_full = _dot(tril3, jnp.concatenate(split3(a_full), axis=0))
        tot_full = acum_full[Q - 1:Q, :]
        xs = xc_s[pl.ds(r0, Q), 0:D_SSM]
        bm = xc_s[pl.ds(r0, Q), D_SSM:D_SSM + N_SSD_GROUPS * N].astype(BF16)
        cm = xc_s[pl.ds(r0, Q), D_SSM + N_SSD_GROUPS * N:CONV_DIM].astype(BF16)
        xd = xs * dt_full
        xdw = (xd * jnp.exp(tot_full - acum_full)).astype(BF16)
        xdb = xd.astype(BF16)

        a_q = expand(a, e_q)
        seg_rhs = jnp.concatenate([a_q, a_q * u_rep], axis=0)
        seg = _dot(seg_lhs3, jnp.concatenate(split3(seg_rhs), axis=0))
        decay = jnp.exp(jnp.where(causal, seg, -jnp.inf))
        cb_parts = []
        for g in range(N_SSD_GROUPS):
            bm_g = bm[:, g * N:(g + 1) * N]
            rep = jnp.concatenate([bm_g] * (H // N_SSD_GROUPS), axis=0)
            cb_parts.append(_dot(cm[:, g * N:(g + 1) * N], rep, _NT))
        mmat = (jnp.concatenate(cb_parts, axis=1) * decay).astype(BF16)

        y_parts = []
        for d in range(n_diag):
            xd_d = xdb[:, d * hp * P:(d + 1) * hp * P]
            bd = jnp.where(bd_mask, jnp.concatenate([xd_d] * hp, axis=0), jnp.zeros((), BF16))
            y_parts.append(_dot(mmat[:, d * hp * Q:(d + 1) * hp * Q], bd))
        y_diag = jnp.concatenate(y_parts, axis=1) if n_diag > 1 else y_parts[0]

        chunk_decay = jnp.exp(tot_full)
        y_off_parts = []
        for g in range(N_SSD_GROUPS):
            st_g = state[:, g * GP:(g + 1) * GP]
            y_off_parts.append(_dot(cm[:, g * N:(g + 1) * N], st_g.astype(BF16)))
            upd = _dot(bm[:, g * N:(g + 1) * N], xdw[:, g * GP:(g + 1) * GP], _TN)
            state[:, g * GP:(g + 1) * GP] = chunk_decay[:, g * GP:(g + 1) * GP] * st_g + upd
        y_off = jnp.concatenate(y_off_parts, axis=1) * jnp.exp(acum_full)

        y = (y_diag + y_off) + dskip_ref[...] * xs
        gt = y * _silu(zs_ref[pl.ds(r0, Q), :])
        for g in range(N_SSD_GROUPS):
            gg = gt[:, g * GP:(g + 1) * GP]
            ms = jnp.mean(gg * gg, axis=-1, keepdims=True)
            y_ref[pl.ds(r0, Q), g * GP:(g + 1) * GP] = (
                gg * lax.rsqrt(ms + RMS_EPS) * nw_ref[:, g * GP:(g + 1) * GP])
        return carry

    lax.fori_loop(0, Lb // Q, chunk, 0)

    @pl.when(i == nblk - 1)
    def _():
        snew_ref[0] = state[...].T


def _ssd(h2d, conv_prev8, ssm_prev, conv_w, conv_b, dt_bias, a_log, dskip_full, norm_w, *, B, L, Q, Lb):
    assert L % Lb == 0 and Lb % Q == 0 and Lb % 8 == 0
    nblk = L // Lb
    HP = N_SSD_HEADS * SSD_HEAD_DIM
    row = lambda b, i: b * nblk + i
    full2 = lambda shp: pl.BlockSpec(shp, lambda b, i: (0, 0))
    return pl.pallas_call(
        functools.partial(_ssd_kernel, Q=Q, Lb=Lb),
        grid=(B, nblk),
        in_specs=[
            pl.BlockSpec((Lb, D_SSM), lambda b, i: (row(b, i), OFF_ZS // D_SSM)),
            pl.BlockSpec((Lb, CONV_DIM), lambda b, i: (row(b, i), OFF_XBC // CONV_DIM)),
            pl.BlockSpec((Lb, 128), lambda b, i: (row(b, i), OFF_SMALL // 128)),
            pl.BlockSpec((1, 8, CONV_DIM), lambda b, i: (b, 0, 0)),
            pl.BlockSpec((1, HP, D_STATE), lambda b, i: (b, 0, 0)),
            full2((CONV_W, CONV_DIM)),
            full2((1, CONV_DIM)),
            full2((1, N_SSD_HEADS)),
            full2((1, N_SSD_HEADS)),
            full2((1, D_SSM)),
            full2((1, D_SSM)),
        ],
        out_specs=[
            pl.BlockSpec((Lb, D_SSM), lambda b, i: (row(b, i), 0)),
            pl.BlockSpec((1, HP, D_STATE), lambda b, i: (b, 0, 0)),
        ],
        out_shape=[
            jax.ShapeDtypeStruct((B * L, D_SSM), F32),
            jax.ShapeDtypeStruct((B, HP, D_STATE), F32),
        ],
        scratch_shapes=[
            pltpu.VMEM((Lb + 8, CONV_DIM), F32),
            pltpu.VMEM((Lb, CONV_DIM), F32),
            pltpu.VMEM((D_STATE, HP), F32),
        ],
        compiler_params=pltpu.CompilerParams(
            dimension_semantics=("arbitrary", "arbitrary"), vmem_limit_bytes=VMEM_LIMIT),
        name="ssd",
    )(h2d, h2d, h2d, conv_prev8, ssm_prev, conv_w, conv_b, dt_bias, a_log, dskip_full, norm_w)


def _alibi_slope(h):
    return float(2.0 ** (-8.0 * (h + 1) / N_ATT_HEADS))


LOG2E = float(np.log2(np.e))
POS_SPLIT = 128
N_POS_ROWS = 6
V_AUG_ROWS = ATT_HEAD_DIM + 16
COUNT_UNROLL = 4


def _alibi_rows():
    rows = np.zeros((N_ATT_HEADS, ATT_HEAD_DIM, 1), np.float32)
    for h in range(N_ATT_HEADS):
        pieces = _split_bf16x3(np.float32(_alibi_slope(h) * LOG2E))
        rows[h, 0:3, 0] = [p * POS_SPLIT for p in pieces]
        rows[h, 3:6, 0] = pieces
    return rows


def _dsa_prompt_kernel(q_ref, qi_ref, sm_ref, arow_ref, kb_ref, vt_ref, kib_ref, o_ref,
                       sc_s, lg_s, p_s, qa_s, m_s, a_s, acc_s, *, Qb, k_sel):
    Kb = Qb
    i = pl.program_id(0)
    nk = i + 1
    rpg = N_ATT_HEADS // N_KV_HEADS
    D = ATT_HEAD_DIM
    DV = V_AUG_ROWS

    q_t = q_ref[...].T * (D ** -0.5 * LOG2E)
    for h in range(N_ATT_HEADS):
        qa_s[h, 0:D, :] = q_t[h * D:(h + 1) * D, :].astype(BF16)
        qa_s[h, D:2 * D, :] = jnp.broadcast_to(arow_ref[h], (D, Qb)).astype(BF16)
    qi_t = qi_ref[...].T.astype(BF16)
    w_t = sm_ref[...].T[SM_WI:SM_WI + IDX_HEADS, :] * (IDX_DIM ** -0.5 * IDX_HEADS ** -0.5)
    qpos = i * Qb + lax.broadcasted_iota(I32, (1, Qb), 1)
    chunk_end = (((qpos >> 6) + 1) << 6) - 1
    krow = lax.broadcasted_iota(I32, (Kb, 1), 0)

    @pl.when(i == 0)
    def _():
        sc_s[...] = jnp.full(sc_s.shape, -jnp.inf, F32)

    def score_blk(j, carry):
        kib = kib_ref[j]
        score = jnp.zeros((Kb, Qb), F32)
        for h in range(IDX_HEADS):
            s = _dot(kib, qi_t[h * IDX_DIM:(h + 1) * IDX_DIM, :])
            score = score + w_t[h:h + 1, :] * jnp.maximum(s, 0.0)
        sc_s[j] = jnp.where((j * Kb + krow) <= chunk_end, score, -jnp.inf)
        return carry

    lax.fori_loop(0, nk, score_blk, 0)

    def count(pred):
        def body(jj, acc):
            for u in range(COUNT_UNROLL):
                hit = jnp.where(pred(sc_s[jj * COUNT_UNROLL + u]), 1, 0).astype(I32)
                acc = acc + hit.reshape(Kb // 32, 32, Qb).sum(axis=0)
            return acc
        trips = lax.shift_right_logical(nk + (COUNT_UNROLL - 1), COUNT_UNROLL.bit_length() - 1)
        acc = lax.fori_loop(0, trips, body, jnp.zeros((32, Qb), I32))
        return acc.sum(axis=0, keepdims=True)

    thr, n_ge, few = _kth_largest(count, k_sel, (1, Qb))
    has_ties = jnp.max(jnp.where(n_ge > k_sel, 1, 0)) > 0
    need = lax.cond(
        has_ties,
        lambda: jnp.where(few, jnp.float32(3e38), (k_sel - count(lambda s: s > thr)).astype(F32)),
        lambda: jnp.zeros((1, Qb), F32))

    m_s[...] = jnp.full(m_s.shape, NEG_BIG, F32)
    acc_s[...] = jnp.zeros(acc_s.shape, F32)

    def selected(sc, seen):
        def no_ties():
            return jnp.where(sc >= thr, 1.0, 0.0), seen

        def with_ties():
            tril =(lax.broadcasted_iota(I32, (Kb, Kb), 1) <= lax.broadcasted_iota(I32, (Kb, Kb), 0))
            eq = sc == thr
            rank = seen + _dot(tril.astype(BF16), jnp.where(eq, 1.0, 0.0).astype(BF16))
            return jnp.where((sc > thr) | (eq & (rank <= need)), 1.0, 0.0), rank[Kb - 1:Kb, :]

        return lax.cond(has_ties, with_ties, no_ties)

    def attend(j, seen, diagonal):
        sel_f, seen = selected(sc_s[j], seen)
        sel = sel_f > 0.5
        kblk = kb_ref[j]
        kpos = j * Kb + krow
        if diagonal:
            adj = jnp.minimum(kpos, 2 * qpos - kpos).astype(F32)
        else:
            lane = lax.broadcasted_iota(I32, (Kb, D), 1)
            hi = (kpos >> 7).astype(F32)
            lo = (kpos & (POS_SPLIT - 1)).astype(F32)
            feat = jnp.where(lane < 3, hi, jnp.where(lane < N_POS_ROWS, lo, 0.0)).astype(BF16)
            kcat = [jnp.concatenate([kblk[:, g * D:(g + 1) * D], feat], axis=1) for g in range(N_KV_HEADS)]
        for h in range(N_ATT_HEADS):
            g = h // rpg
            if diagonal:
                lg = _dot(kblk[:, g * D:(g + 1) * D], qa_s[h, 0:D, :]) + (_alibi_slope(h) * LOG2E) * adj
            else:
                lg = _dot(kcat[g], qa_s[h])
            lg = jnp.where(sel, lg, -jnp.inf)
            lg_s[h] = lg
            m_old = m_s[h:h + 1, :]
            m_new = jnp.maximum(m_old, jnp.max(lg, axis=0, keepdims=True))
            a_s[h:h + 1, :] = jnp.exp2(m_old - m_new)
            m_s[h:h + 1, :] = m_new
        for h in range(N_ATT_HEADS):
            p_s[h] = jnp.exp2(lg_s[h] - m_s[h:h + 1, :]).astype(BF16)
        vblk = vt_ref[j]
        for h in range(N_ATT_HEADS):
            g = h // rpg
            acc_s[h] = a_s[h:h + 1, :] * acc_s[h] + _dot(vblk[g * DV:(g + 1) * DV, :], p_s[h])
        return seen

    seen = lax.fori_loop(0, i, lambda j, s: attend(j, s, False), jnp.zeros((1, Qb), F32))
    attend(i, seen, True)

    for h in range(N_ATT_HEADS):
        out_t = acc_s[h, 0:D, :] * (1.0 / acc_s[h, D:D + 1, :])
        o_ref[:, h * D:(h + 1) * D] = out_t.T


def _dsa_prompt(h2d, kb3, vt3, kib3, *, S, Qb, k_sel):
    nkb = S // Qb
    const3 = lambda shp: pl.BlockSpec(shp, lambda i: (0, 0, 0), pipeline_mode=pl.Buffered(1))
    return pl.pallas_call(
        functools.partial(_dsa_prompt_kernel, Qb=Qb, k_sel=k_sel),
        grid=(nkb,),
        in_specs=[
            pl.BlockSpec((Qb, D_ATT), lambda i: (i, OFF_Q // D_ATT)),
            pl.BlockSpec((Qb, IDX_HEADS * IDX_DIM), lambda i: (i, OFF_QI // (IDX_HEADS * IDX_DIM))),
            pl.BlockSpec((Qb, 128), lambda i: (i, OFF_SMALL // 128)),
            const3((N_ATT_HEADS, ATT_HEAD_DIM, 1)),
            const3((nkb, Qb, KV_DIM)),
            const3((nkb, N_KV_HEADS * V_AUG_ROWS, Qb)),
            const3((nkb, Qb, IDX_DIM)),
        ],
        out_specs=pl.BlockSpec((Qb, D_ATT), lambda i: (i, 0)),
        out_shape=jax.ShapeDtypeStruct((S, D_ATT), F32),
        scratch_shapes=[
            pltpu.VMEM((nkb + COUNT_UNROLL - 1, Qb, Qb), F32),
            pltpu.VMEM((N_ATT_HEADS, Qb, Qb), F32),
            pltpu.VMEM((N_ATT_HEADS, Qb, Qb), BF16),
            pltpu.VMEM((N_ATT_HEADS, 2 * ATT_HEAD_DIM, Qb), BF16),
            pltpu.VMEM((N_ATT_HEADS, Qb), F32),
            pltpu.VMEM((N_ATT_HEADS, Qb), F32),
            pltpu.VMEM((N_ATT_HEADS, V_AUG_ROWS, Qb), F32),
        ],
        compiler_params=pltpu.CompilerParams(
            dimension_semantics=("arbitrary",), vmem_limit_bytes=VMEM_LIMIT),
        name="dsa_prompt",
    )(h2d, h2d, h2d, jnp.asarray(_alibi_rows()), kb3, vt3, kib3)


def _dsa_sample_kernel(qr_ref, qir_ref, wr_ref, slope_ref, ck_ref, cv_ref, cki_ref, kn_ref, vn_ref, kin_ref,
                       o_ref, keys_s, sel_s, *, T, P, k_sel):
    LN = 128
    L = P + LN
    rpg = N_ATT_HEADS // N_KV_HEADS
    R = rpg * T

    qi_b = qir_ref[0].astype(BF16)
    w_col = wr_ref[0] * (IDX_DIM ** -0.5 * IDX_HEADS ** -0.5)

    def head_sum(s):
        return (jnp.maximum(s, 0.0) * w_col).reshape(IDX_HEADS, T, s.shape[-1]).sum(axis=0)

    sc_c = head_sum(_dot(qi_b, cki_ref[0].astype(BF16), _NT))
    sc_n = head_sum(_dot(qi_b, kin_ref[0].astype(BF16), _NT))
    keys_s[:, 0:P] = sc_c
    col_n = lax.broadcasted_iota(I32, (T, LN), 1)
    keys_s[:, P:L] = jnp.where(col_n < T, sc_n, -jnp.inf)

    def count(pred):
        return jnp.sum(jnp.where(pred(keys_s[...]), 1, 0).astype(I32), axis=1, keepdims=True)

    thr, _, few = _kth_largest(count, k_sel, (T, 1))
    need = jnp.where(few, jnp.float32(3e38), (k_sel - count(lambda s: s > thr)).astype(F32))

    triu = (lax.broadcasted_iota(I32, (LN, LN), 0) <= lax.broadcasted_iota(I32, (LN, LN), 1)).astype(BF16)
    seen = jnp.zeros((T, 1), F32)
    for jb in range(L // LN):
        kk = keys_s[:, jb * LN:(jb + 1) * LN]
        eq = kk == thr
        rank = seen + _dot(jnp.where(eq, 1.0, 0.0).astype(BF16), triu)
        sel_s[:, jb * LN:(jb + 1) * LN] = jnp.where((kk > thr) | (eq & (rank <= need)), 1.0, 0.0)
        seen = rank[:, LN - 1:LN]

    qpos = P + lax.broadcasted_iota(I32, (T, L), 0)
    kpos = lax.broadcasted_iota(I32, (T, L), 1)
    dist = jnp.abs(qpos - kpos).astype(F32)
    dist_r = jnp.concatenate([dist] * rpg, axis=0)
    sel_r = jnp.concatenate([sel_s[...]] * rpg, axis=0) > 0.5
    for g in range(N_KV_HEADS):
        sl = slice(g * ATT_HEAD_DIM, (g + 1) * ATT_HEAD_DIM)
        qg = (qr_ref[0, g * R:(g + 1) * R, :] * (ATT_HEAD_DIM ** -0.5)).astype(BF16)
        cache_rows = pl.ds(g, P, stride=N_KV_HEADS)
        lg = jnp.concatenate([_dot(qg, ck_ref[0, cache_rows, :].astype(BF16), _NT),
                              _dot(qg, kn_ref[0, :, sl].astype(BF16), _NT)], axis=1)
        lg = jnp.where(sel_r, lg - slope_ref[g * R:(g + 1) * R, :] * dist_r, -jnp.inf)
        m = jnp.max(lg, axis=1, keepdims=True)
        p = jnp.exp(lg - m)
        den = jnp.sum(p, axis=1, keepdims=True)
        pb = p.astype(BF16)
        out = (_dot(pb[:, 0:P], cv_ref[0, cache_rows, :].astype(BF16))
               + _dot(pb[:, P:L], vn_ref[0, :, sl].astype(BF16)))
        out = out * (1.0 / den)
        for r in range(rpg):
            hh = g * rpg + r
            o_ref[0, :, hh * ATT_HEAD_DIM:(hh + 1) * ATT_HEAD_DIM] = out[r * T:(r + 1) * T, :]


def _dsa_sample(q_rows, qi_rows, w_rows, slope_rows, ck, cv, cki, kn, vn, kin, *, Bd, T, P, k_sel):
    LN = 128
    per_b = lambda shp: pl.BlockSpec(shp, lambda b: (b, 0, 0))
    return pl.pallas_call(
        functools.partial(_dsa_sample_kernel, T=T, P=P, k_sel=k_sel),
        grid=(Bd,),
        in_specs=[
            per_b((1, N_ATT_HEADS * T, ATT_HEAD_DIM)),
            per_b((1, IDX_HEADS * T, IDX_DIM)),
            per_b((1, IDX_HEADS * T, 1)),
            pl.BlockSpec((N_ATT_HEADS * T, 1), lambda b: (0, 0)),
            per_b((1, N_KV_HEADS * P, ATT_HEAD_DIM)),
            per_b((1, N_KV_HEADS * P, ATT_HEAD_DIM)),
            per_b((1, P, IDX_DIM)),
            per_b((1, LN, KV_DIM)),
            per_b((1, LN, KV_DIM)),
            per_b((1, LN, IDX_DIM)),
        ],
        out_specs=per_b((1, T, D_ATT)),
        out_shape=jax.ShapeDtypeStruct((Bd, T, D_ATT), F32),
        scratch_shapes=[
            pltpu.VMEM((T, P + LN), F32),
            pltpu.VMEM((T, P + LN), F32),
        ],
        compiler_params=pltpu.CompilerParams(
            dimension_semantics=("arbitrary",), vmem_limit_bytes=VMEM_LIMIT),
        name="dsa_sample",
    )(q_rows, qi_rows, w_rows, slope_rows, ck, cv, cki, kn, vn, kin)


def _merge_kernel(x_ref, ys_ref, att_ref, za_ref, wo_ref, g_ref, b_ref, o_ref):
    att = att_ref[...] * _silu(za_ref[...])
    mix = _dot(ys_ref[...].astype(BF16), wo_ref[0:D_SSM, :]) + _dot(att.astype(BF16), wo_ref[D_SSM:, :])
    hres = ALPHA * x_ref[...] + mix
    mu = jnp.mean(hres, axis=-1, keepdims=True)
    hc = hres - mu
    var = jnp.mean(hc * hc, axis=-1, keepdims=True)
    o_ref[...] = hc * lax.rsqrt(var + LN_EPS) * g_ref[...] + b_ref[...]


def _merge(x2d, y_ssd, att, h2d, w_out_b, ln_g, ln_b, tm):
    m = x2d.shape[0]
    rows = lambda i: (i, 0)
    return pl.pallas_call(
        _merge_kernel,
        grid=(m // tm,),
        in_specs=[
            pl.BlockSpec((tm, D_MODEL), rows),
            pl.BlockSpec((tm, D_SSM), rows),
            pl.BlockSpec((tm, D_ATT), rows),
            pl.BlockSpec((tm, D_ATT), lambda i: (i, OFF_ZA // D_ATT)),
            pl.BlockSpec((D_SSM + D_ATT, D_MODEL), lambda i: (0, 0), pipeline_mode=pl.Buffered(1)),
            pl.BlockSpec((1, D_MODEL), lambda i: (0, 0)),
            pl.BlockSpec((1, D_MODEL), lambda i: (0, 0)),
        ],
        out_specs=pl.BlockSpec((tm, D_MODEL), rows),
        out_shape=jax.ShapeDtypeStruct((m, D_MODEL), F32),
        compiler_params=pltpu.CompilerParams(
            dimension_semantics=("arbitrary",), vmem_limit_bytes=VMEM_LIMIT),
        name="merge",
    )(x2d, y_ssd, att, h2d, w_out_b, ln_g, ln_b)


def _regroup_w_in(w_in):
    s = _SRC
    cols = [w_in[:, s["zs"]:s["zs"] + 1024], w_in[:, s["q"]:s["q"] + 1024], w_in[:, s["za"]:s["za"] + 1024],
            w_in[:, s["xbc"]:s["xbc"] + CONV_DIM], w_in[:, s["qi"]:s["qi"] + 512],
            w_in[:, s["k"]:s["k"] + KV_DIM], w_in[:, s["v"]:s["v"] + KV_DIM],
            w_in[:, s["dt"]:s["dt"] + 16], w_in[:, s["ki"]:s["ki"] + IDX_DIM], w_in[:, s["wi"]:s["wi"] + IDX_HEADS],
            jnp.zeros((D_MODEL, 128 - 16 - IDX_DIM - IDX_HEADS), w_in.dtype)]
    return jnp.concatenate(cols, axis=1).astype(BF16)


def _row_tile(m):
    return 256 if m % 256 == 0 else m


def kernel(x_prompt, x_sample, cache_k, cache_v, cache_kidx, state_ssm, state_conv, w_in, conv_w, conv_b,
           dt_bias, a_log, d_skip, ssd_norm_w, w_out, ln_g, ln_b):
    B, S, _ = x_prompt.shape
    Bd, T, _ = x_sample.shape
    P = cache_k.shape[1]
    assert B == 1 and S % 256 == 0 and T % 8 == 0 and T >= CONV_W - 1 and T <= 128 and P % 128 == 0
    HP = N_SSD_HEADS * SSD_HEAD_DIM

    w_pad = _regroup_w_in(w_in)
    w_out_b = w_out.astype(BF16)
    conv_b2 = conv_b.reshape(1, CONV_DIM)
    dtb2 = dt_bias.reshape(1, N_SSD_HEADS)
    alog2 = a_log.reshape(1, N_SSD_HEADS)
    dskip_full = jnp.repeat(d_skip, SSD_HEAD_DIM).reshape(1, D_SSM)
    nw2 = ssd_norm_w.reshape(1, D_SSM)
    g2 = ln_g.reshape(1, D_MODEL)
    b2 = ln_b.reshape(1, D_MODEL)

    xp = x_prompt.reshape(S, D_MODEL)
    Qb = _row_tile(S)
    h_p, k_p, v_p, ki_p, kb3, vt3, kib3 = _project(xp, w_pad, Qb)
    conv0 = jnp.zeros((1, 8, CONV_DIM), F32)
    ssm0 = jnp.zeros((1, HP, D_STATE), F32)
    yssd_p, ssm_p = _ssd(h_p, conv0, ssm0, conv_w, conv_b2, dtb2, alog2, dskip_full, nw2,
                         B=1, L=S, Q=CHUNK, Lb=256)
    att_p = _dsa_prompt(h_p, kb3, vt3, kib3, S=S, Qb=Qb, k_sel=min(TOPK_MAX, S // 4))
    y_p = _merge(xp, yssd_p, att_p, h_p, w_out_b, g2, b2, _row_tile(S))

    M = Bd * T
    xs = x_sample.reshape(M, D_MODEL)
    h_s, k_s, v_s, ki_s = _project(xs, w_pad, _row_tile(M))[:4]
    conv_prev8 = jnp.pad(state_conv, ((0, 0), (8 - (CONV_W - 1), 0), (0, 0)))
    yssd_s, ssm_s = _ssd(h_s, conv_prev8, state_ssm.reshape(Bd, HP, D_STATE), conv_w, conv_b2, dtb2, alog2,
                         dskip_full, nw2, B=Bd, L=T, Q=T, Lb=T)
    h_s3 = h_s.reshape(Bd, T, D_PAD)

    def head_rows(a, nh, dh):
        return a.reshape(Bd, T, nh, dh).transpose(0, 2, 1, 3).reshape(Bd, nh * T, dh)

    q_rows = head_rows(h_s3[:, :, OFF_Q:OFF_Q + D_ATT], N_ATT_HEADS, ATT_HEAD_DIM)
    qi_rows = head_rows(h_s3[:, :, OFF_QI:OFF_QI + IDX_HEADS * IDX_DIM], IDX_HEADS, IDX_DIM)
    w_rows = head_rows(h_s3[:, :, OFF_SMALL + SM_WI:OFF_SMALL + SM_WI + IDX_HEADS], IDX_HEADS, 1)
    slope_rows = jnp.asarray(np.repeat([_alibi_slope(h) for h in range(N_ATT_HEADS)], T).reshape(-1, 1), F32)
    padn = lambda a: jnp.pad(a.reshape(Bd, T, -1), ((0, 0), (0, 128 - T), (0, 0)))
    att_s = _dsa_sample(q_rows, qi_rows, w_rows, slope_rows,
                        cache_k.reshape(Bd, N_KV_HEADS * P, ATT_HEAD_DIM),
                        cache_v.reshape(Bd, N_KV_HEADS * P, ATT_HEAD_DIM), cache_kidx,
                        padn(k_s), padn(v_s), padn(ki_s), Bd=Bd, T=T, P=P, k_sel=min(TOPK_MAX, (P + T) // 4))
    y_s = _merge(xs, yssd_s, att_s.reshape(M, D_ATT), h_s, w_out_b, g2, b2, _row_tile(M))

    kv4 = lambda a, b_, l: a.reshape(b_, l, N_KV_HEADS, ATT_HEAD_DIM)
    st4 = lambda a, b_: a.reshape(b_, N_SSD_HEADS, SSD_HEAD_DIM, D_STATE)
    conv_p = h_p[S - (CONV_W - 1):, OFF_XBC:OFF_XBC + CONV_DIM].reshape(1, CONV_W - 1, CONV_DIM)
    conv_s = h_s3[:, T - (CONV_W - 1):, OFF_XBC:OFF_XBC + CONV_DIM]
    return (y_p.reshape(1, S, D_MODEL), y_s.reshape(Bd, T, D_MODEL),
            kv4(k_p, 1, S), kv4(v_p, 1, S), ki_p.reshape(1, S, IDX_DIM), st4(ssm_p, 1), conv_p,
            kv4(k_s, Bd, T), kv4(v_s, Bd, T), ki_s.reshape(Bd, T, IDX_DIM), st4(ssm_s, Bd), conv_s)
```

```python
import functools

import numpy as np
import jax
import jax.numpy as jnp
from jax import lax
from jax.experimental import pallas as pl
from jax.experimental.pallas import tpu as pltpu

F32 = jnp.float32
BF16 = jnp.bfloat16
I32 = jnp.int32

D_MODEL = 1024
CHUNK = 64
D_SSM = 1024
SSD_HEAD_DIM = 64
N_SSD_HEADS = 16
N_SSD_GROUPS = 2
D_STATE = 128
CONV_W = 4
CONV_DIM = D_SSM + 2 * N_SSD_GROUPS * D_STATE
D_ATT = 1024
ATT_HEAD_DIM = 128
N_ATT_HEADS = 8
N_KV_HEADS = 2
KV_DIM = N_KV_HEADS * ATT_HEAD_DIM
IDX_HEADS = 8
IDX_DIM = 64
TOPK_MAX = 256
ALPHA = 2.0 ** 0.25
LN_EPS = 1e-5
RMS_EPS = 1e-5

_SRC = dict(zs=0, xbc=1024, dt=2560, q=2576, k=3600, v=3856, za=4112, qi=5136, ki=5648, wi=5712)
OFF_ZS, OFF_Q, OFF_ZA, OFF_XBC, OFF_QI, OFF_K, OFF_V, OFF_SMALL = 0, 1024, 2048, 3072, 4608, 5120, 5376, 5632
D_PAD = 5760
SM_DT, SM_KI, SM_WI = 0, 16, 80

NEG_BIG = -1e30

VMEM_LIMIT = 56 * 1024 * 1024


def _dot(a, b, dims=(((1,), (0,)), ((), ())), precision=None):
    return lax.dot_general(a, b, dims, precision=precision, preferred_element_type=F32)


_NT = (((1,), (1,)), ((), ()))
_TN = (((0,), (0,)), ((), ()))


def _silu(x):
    return x * (1.0 / (1.0 + jnp.exp(-x)))


def _softplus(x):
    return jnp.maximum(x, 0.0) + jnp.log1p(jnp.exp(-jnp.abs(x)))


F32_LOWEST = float(np.finfo(np.float32).min)
KEY_LOWEST = int(np.array(F32_LOWEST, np.float32).view(np.int32)) ^ 0x7FFFFFFF
BRACKET = 0.125


def _float_to_key(x):
    b = pltpu.bitcast(x, I32)
    return b ^ ((b >> 31) & jnp.int32(0x7FFFFFFF))


def _key_to_float(k):
    return pltpu.bitcast(k ^ ((k >> 31) & jnp.int32(0x7FFFFFFF)), F32)


def _kth_largest(count, k_sel, s_max, n_valid):
    few = n_valid < k_sel
    guess = jnp.where(s_max > 0, s_max * BRACKET, s_max * (1.0 / BRACKET))
    n_guess = count(lambda s: s >= guess)
    good = n_guess >= k_sel
    lo = jnp.where(good, _float_to_key(guess), jnp.int32(KEY_LOWEST))
    n_ge = jnp.where(good, n_guess, n_valid)
    hi = jnp.where(few, lo, jnp.maximum(_float_to_key(s_max), lo))
    n_pass = jnp.max(33 - lax.clz((hi >> 1) - (lo >> 1)))

    def step(_, carry):
        lo, hi, n_ge = carry
        x = lo ^ hi
        mid = (lo & hi) + (x >> 1) + (x & 1)
        cnt = count(lambda s: s >= _key_to_float(mid))
        ok = cnt >= k_sel
        return jnp.where(ok, mid, lo), jnp.where(ok, hi, mid - 1), jnp.where(ok, cnt, n_ge)

    lo, hi, n_ge = lax.fori_loop(0, n_pass, step, (lo, hi, n_ge))
    thr = jnp.where(few, jnp.float32(F32_LOWEST), _key_to_float(lo))
    return thr, jnp.where(few, k_sel, n_ge), few


def _split_bf16x3(x):
    parts, r = [], np.float32(x)
    for _ in range(3):
        p = np.float32(np.asarray(r, dtype=BF16))
        parts.append(float(p))
        r = np.float32(r - p)
    assert r == 0.0
    return parts


_PROJ_CHUNKS = tuple((c, 512) for c in range(0, 5632, 512)) + ((5632, 128),)


def _proj_kernel(x_ref, w_ref, h_ref, k_ref, v_ref, ki_ref, kb_ref, vt_ref, kib_ref):
    xb = x_ref[...].astype(BF16)
    for c0, cw in _PROJ_CHUNKS:
        h_ref[:, c0:c0 + cw] = _dot(xb, w_ref[:, c0:c0 + cw])
    k = h_ref[:, OFF_K:OFF_K + KV_DIM]
    v = h_ref[:, OFF_V:OFF_V + KV_DIM]
    ki = h_ref[:, OFF_SMALL + SM_KI:OFF_SMALL + SM_KI + IDX_DIM]
    tm = k.shape[0]
    for g in range(N_KV_HEADS):
        k_ref[pl.ds(g, tm, stride=N_KV_HEADS), :] = k[:, g * ATT_HEAD_DIM:(g + 1) * ATT_HEAD_DIM]
        v_ref[pl.ds(g, tm, stride=N_KV_HEADS), :] = v[:, g * ATT_HEAD_DIM:(g + 1) * ATT_HEAD_DIM]
    ki_ref[...] = ki
    kb_ref[0] = k.astype(BF16)
    kib_ref[0] = ki.astype(BF16)
    v_t = v.T
    for g in range(N_KV_HEADS):
        r0 = g * V_AUG_ROWS
        vt_ref[0, r0:r0 + ATT_HEAD_DIM, :] = v_t[g * ATT_HEAD_DIM:(g + 1) * ATT_HEAD_DIM, :].astype(BF16)
        vt_ref[0, r0 + ATT_HEAD_DIM:r0 + V_AUG_ROWS, :] = jnp.ones(
            (V_AUG_ROWS - ATT_HEAD_DIM, v_t.shape[1]), BF16)


def _project(x2d, w_pad, tm):
    m = x2d.shape[0]
    assert m % tm == 0
    return pl.pallas_call(
        _proj_kernel,
        grid=(m // tm,),
        in_specs=[
            pl.BlockSpec((tm, D_MODEL), lambda i: (i, 0)),
            pl.BlockSpec((D_MODEL, D_PAD), lambda i: (0, 0), pipeline_mode=pl.Buffered(1)),
        ],
        out_specs=[
            pl.BlockSpec((tm, D_PAD), lambda i: (i, 0)),
            pl.BlockSpec((N_KV_HEADS * tm, ATT_HEAD_DIM), lambda i: (i, 0)),
            pl.BlockSpec((N_KV_HEADS * tm, ATT_HEAD_DIM), lambda i: (i, 0)),
            pl.BlockSpec((tm, IDX_DIM), lambda i: (i, 0)),
            pl.BlockSpec((1, tm, KV_DIM), lambda i: (i, 0, 0)),
            pl.BlockSpec((1, N_KV_HEADS * V_AUG_ROWS, tm), lambda i: (i, 0, 0)),
            pl.BlockSpec((1, tm, IDX_DIM), lambda i: (i, 0, 0)),
        ],
        out_shape=[
            jax.ShapeDtypeStruct((m, D_PAD), F32),
            jax.ShapeDtypeStruct((N_KV_HEADS * m, ATT_HEAD_DIM), F32),
            jax.ShapeDtypeStruct((N_KV_HEADS * m, ATT_HEAD_DIM), F32),
            jax.ShapeDtypeStruct((m, IDX_DIM), F32),
            jax.ShapeDtypeStruct((m // tm, tm, KV_DIM), BF16),
            jax.ShapeDtypeStruct((m // tm, N_KV_HEADS * V_AUG_ROWS, tm), BF16),
            jax.ShapeDtypeStruct((m // tm, tm, IDX_DIM), BF16),
        ],
        compiler_params=pltpu.CompilerParams(
            dimension_semantics=("arbitrary",), vmem_limit_bytes=VMEM_LIMIT),
        name="proj",
    )(x2d, w_pad)


def _ssd_kernel(zs_ref, xbc_ref, sm_ref, cprev_ref, sprev_ref, cw_ref, cb_ref, dtb_ref, alog_ref,
                dskip_ref, nw_ref, y_ref, snew_ref, xpad, xc_s, state, *, Q, Lb):
    i = pl.program_id(1)
    nblk = pl.num_programs(1)
    H, P, N = N_SSD_HEADS, SSD_HEAD_DIM, D_STATE
    HQ = H * Q
    hp = min(H, 256 // Q)
    n_diag = H // hp
    GP = (H // N_SSD_GROUPS) * P

    @pl.when(i == 0)
    def _():
        xpad[0:8, :] = cprev_ref[0]
        state[...] = sprev_ref[0].T

    @pl.when(i > 0)
    def _():
        xpad[0:8, :] = xpad[Lb:Lb + 8, :]

    xpad[8:8 + Lb, :] = xbc_ref[...]
    conv = cb_ref[...] + xpad[5:5 + Lb, :] * cw_ref[0:1, :]
    conv = conv + xpad[6:6 + Lb, :] * cw_ref[1:2, :]
    conv = conv + xpad[7:7 + Lb, :] * cw_ref[2:3, :]
    conv = conv + xpad[8:8 + Lb, :] * cw_ref[3:4, :]
    xc_s[...] = _silu(conv)

    a_neg = -jnp.exp(alog_ref[...])

    def iota(shape, d):
        return lax.broadcasted_iota(I32, shape, d)

    lq = Q.bit_length() - 1
    e_p = (iota((H, H * P), 1) >> 6 == iota((H, H * P), 0)).astype(BF16)
    e_q = (iota((H, HQ), 1) >> lq == iota((H, HQ), 0)).astype(BF16)
    j_of = iota((Q, HQ), 1) & (Q - 1)
    t_of = iota((Q, HQ), 0)
    u_rep = (t_of <= j_of).astype(F32)
    causal = j_of <= t_of
    c6 = iota((Q, 6 * Q), 1) & (2 * Q - 1)
    r6 = iota((Q, 6 * Q), 0)
    seg_lhs3 = jnp.where(c6 < Q, (c6 <= r6).astype(F32), -1.0).astype(BF16)
    tril3 = ((iota((Q, 3 * Q), 1) & (Q - 1)) <= iota((Q, 3 * Q), 0)).astype(BF16)
    bd_mask = (iota((hp * Q, hp * P), 0) >> lq) == (iota((hp * Q, hp * P), 1) >> 6)

    def split3(x):
        x1 = x.astype(BF16)
        r1 = x - x1.astype(F32)
        x2 = r1.astype(BF16)
        return x1, x2, (r1 - x2.astype(F32)).astype(BF16)

    def expand(x, e):
        rows = x.shape[0]
        r = _dot(jnp.concatenate(split3(x), axis=0), e)
        return (r[0:rows] + r[rows:2 * rows]) + r[2 * rows:3 * rows]

    def chunk(c, carry):
        r0 = pl.multiple_of(c * Q, Q)
        dt = _softplus(sm_ref[pl.ds(r0, Q), SM_DT:SM_DT + H] + dtb_ref[...])
        a = dt * a_neg
        both = expand(jnp.concatenate([dt, a], axis=0), e_p)
        dt_full, a_full = both[:Q], both[Q:]
        acum_full = _dot(tril3, jnp.concatenate(split3(a_full), axis=0))
        tot_full = acum_full[Q - 1:Q, :]
        xs = xc_s[pl.ds(r0, Q), 0:D_SSM]
        bm = xc_s[pl.ds(r0, Q), D_SSM:D_SSM + N_SSD_GROUPS * N].astype(BF16)
        cm = xc_s[pl.ds(r0, Q), D_SSM + N_SSD_GROUPS * N:CONV_DIM].astype(BF16)
        xd = xs * dt_full
        xdw = (xd * jnp.exp(tot_full - acum_full)).astype(BF16)
        xdb = xd.astype(BF16)

        a_q = expand(a, e_q)
        seg_rhs = jnp.concatenate([a_q, a_q * u_rep], axis=0)
        seg = _dot(seg_lhs3, jnp.concatenate(split3(seg_rhs), axis=0))
        decay = jnp.exp(jnp.where(causal, seg, -jnp.inf))
        cb_parts = []
        for g in range(N_SSD_GROUPS):
            bm_g = bm[:, g * N:(g + 1) * N]
            rep = jnp.concatenate([bm_g] * (H // N_SSD_GROUPS), axis=0)
            cb_parts.append(_dot(cm[:, g * N:(g + 1) * N], rep, _NT))
        mmat = (jnp.concatenate(cb_parts, axis=1) * decay).astype(BF16)

        y_parts = []
        for d in range(n_diag):
            xd_d = xdb[:, d * hp * P:(d + 1) * hp * P]
            bd = jnp.where(bd_mask, jnp.concatenate([xd_d] * hp, axis=0), jnp.zeros((), BF16))
            y_parts.append(_dot(mmat[:, d * hp * Q:(d + 1) * hp * Q], bd))
        y_diag = jnp.concatenate(y_parts, axis=1) if n_diag > 1 else y_parts[0]

        chunk_decay = jnp.exp(tot_full)
        y_off_parts = []
        for g in range(N_SSD_GROUPS):
            st_g = state[:, g * GP:(g + 1) * GP]
            y_off_parts.append(_dot(cm[:, g * N:(g + 1) * N], st_g.astype(BF16)))
            upd = _dot(bm[:, g * N:(g + 1) * N], xdw[:, g * GP:(g + 1) * GP], _TN)
            state[:, g * GP:(g + 1) * GP] = chunk_decay[:, g * GP:(g + 1) * GP] * st_g + upd
        y_off = jnp.concatenate(y_off_parts, axis=1) * jnp.exp(acum_full)

        y = (y_diag + y_off) + dskip_ref[...] * xs
        gt = y * _silu(zs_ref[pl.ds(r0, Q), :])
        for g in range(N_SSD_GROUPS):
            gg = gt[:, g * GP:(g + 1) * GP]
            ms = jnp.mean(gg * gg, axis=-1, keepdims=True)
            y_ref[pl.ds(r0, Q), g * GP:(g + 1) * GP] = (
                gg * lax.rsqrt(ms + RMS_EPS) * nw_ref[:, g * GP:(g + 1) * GP])
        return carry

    lax.fori_loop(0, Lb // Q, chunk, 0)

    @pl.when(i == nblk - 1)
    def _():
        snew_ref[0] = state[...].T


def _ssd(h2d, conv_prev8, ssm_prev, conv_w, conv_b, dt_bias, a_log, dskip_full, norm_w, *, B, L, Q, Lb):
    assert L % Lb == 0 and Lb % Q == 0 and Lb % 8 == 0
    nblk = L // Lb
    HP = N_SSD_HEADS * SSD_HEAD_DIM
    row = lambda b, i: b * nblk + i
    full2 = lambda shp: pl.BlockSpec(shp, lambda b, i: (0, 0))
    return pl.pallas_call(
        functools.partial(_ssd_kernel, Q=Q, Lb=Lb),
        grid=(B, nblk),
        in_specs=[
            pl.BlockSpec((Lb, D_SSM), lambda b, i: (row(b, i), OFF_ZS // D_SSM)),
            pl.BlockSpec((Lb, CONV_DIM), lambda b, i: (row(b, i), OFF_XBC // CONV_DIM)),
            pl.BlockSpec((Lb, 128), lambda b, i: (row(b, i), OFF_SMALL // 128)),
            pl.BlockSpec((1, 8, CONV_DIM), lambda b, i: (b, 0, 0)),
            pl.BlockSpec((1, HP, D_STATE), lambda b, i: (b, 0, 0)),
            full2((CONV_W, CONV_DIM)),
            full2((1, CONV_DIM)),
            full2((1, N_SSD_HEADS)),
            full2((1, N_SSD_HEADS)),
            full2((1, D_SSM)),
            full2((1, D_SSM)),
        ],
        out_specs=[
            pl.BlockSpec((Lb, D_SSM), lambda b, i: (row(b, i), 0)),
            pl.BlockSpec((1, HP, D_STATE), lambda b, i: (b, 0, 0)),
        ],
        out_shape=[
            jax.ShapeDtypeStruct((B * L, D_SSM), F32),
            jax.ShapeDtypeStruct((B, HP, D_STATE), F32),
        ],
        scratch_shapes=[
            pltpu.VMEM((Lb + 8, CONV_DIM), F32),
            pltpu.VMEM((Lb, CONV_DIM), F32),
            pltpu.VMEM((D_STATE, HP), F32),
        ],
        compiler_params=pltpu.CompilerParams(
            dimension_semantics=("arbitrary", "arbitrary"), vmem_limit_bytes=VMEM_LIMIT),
        name="ssd",
    )(h2d, h2d, h2d, conv_prev8, ssm_prev, conv_w, conv_b, dt_bias, a_log, dskip_full, norm_w)


def _alibi_slope(h):
    return float(2.0 ** (-8.0 * (h + 1) / N_ATT_HEADS))


LOG2E = float(np.log2(np.e))
POS_SPLIT = 128
N_POS_ROWS = 6
V_AUG_ROWS = ATT_HEAD_DIM + 16
COUNT_UNROLL = 4


def _alibi_rows():
    rows = np.zeros((N_ATT_HEADS, ATT_HEAD_DIM, 1), np.float32)
    for h in range(N_ATT_HEADS):
        pieces = _split_bf16x3(np.float32(_alibi_slope(h) * LOG2E))
        rows[h, 0:3, 0] = [p * POS_SPLIT for p in pieces]
        rows[h, 3:6, 0] = pieces
    return rows


def _dsa_prompt_kernel(q_ref, qi_ref, sm_ref, arow_ref, kb_ref, vt_ref, kib_ref, o_ref,
                       sc_s, lg_s, p_s, qa_s, m_s, a_s, acc_s, *, Qb, k_sel):
    Kb = Qb
    i = pl.program_id(0)
    nk = i + 1
    rpg = N_ATT_HEADS // N_KV_HEADS
    D = ATT_HEAD_DIM
    DV = V_AUG_ROWS

    q_t = q_ref[...].T * (D ** -0.5 * LOG2E)
    for h in range(N_ATT_HEADS):
        qa_s[h, 0:D, :] = q_t[h * D:(h + 1) * D, :].astype(BF16)
        qa_s[h, D:2 * D, :] = jnp.broadcast_to(arow_ref[h], (D, Qb)).astype(BF16)
    qi_t = qi_ref[...].T.astype(BF16)
    w_t = sm_ref[...].T[SM_WI:SM_WI + IDX_HEADS, :] * (IDX_DIM ** -0.5 * IDX_HEADS ** -0.5)
    qpos = i * Qb + lax.broadcasted_iota(I32, (1, Qb), 1)
    chunk_end = (((qpos >> 6) + 1) << 6) - 1
    krow = lax.broadcasted_iota(I32, (Kb, 1), 0)

    @pl.when(i == 0)
    def _():
        sc_s[...] = jnp.full(sc_s.shape, -jnp.inf, F32)

    def score_blk(j, carry):
        kib = kib_ref[j]
        score = jnp.zeros((Kb, Qb), F32)
        for h in range(IDX_HEADS):
            s = _dot(kib, qi_t[h * IDX_DIM:(h + 1) * IDX_DIM, :])
            score = score + w_t[h:h + 1, :] * jnp.maximum(s, 0.0)
        score = jnp.where((j * Kb + krow) <= chunk_end, score, -jnp.inf)
        sc_s[j] = score
        s_max, n_valid = carry
        s_max = jnp.maximum(s_max, score.reshape(Kb // 32, 32, Qb).max(axis=0))
        valid = jnp.where(score > -jnp.inf, 1, 0).astype(I32)
        return s_max, n_valid + valid.reshape(Kb // 32, 32, Qb).sum(axis=0)

    s_max, n_valid = lax.fori_loop(
        0, nk, score_blk, (jnp.full((32, Qb), -jnp.inf, F32), jnp.zeros((32, Qb), I32)))
    s_max = s_max.max(axis=0, keepdims=True)
    n_valid = n_valid.sum(axis=0, keepdims=True)

    def count(pred):
        def body(jj, acc):
            for u in range(COUNT_UNROLL):
                hit = jnp.where(pred(sc_s[jj * COUNT_UNROLL + u]), 1, 0).astype(I32)
                acc = acc + hit.reshape(Kb // 32, 32, Qb).sum(axis=0)
            return acc
        trips = lax.shift_right_logical(nk + (COUNT_UNROLL - 1), COUNT_UNROLL.bit_length() - 1)
        acc = lax.fori_loop(0, trips, body, jnp.zeros((32, Qb), I32))
        return acc.sum(axis=0, keepdims=True)

    thr, n_ge, few = _kth_largest(count, k_sel, s_max, n_valid)
    has_ties = jnp.max(jnp.where(n_ge > k_sel, 1, 0)) > 0
    need = lax.cond(
        has_ties,
        lambda: jnp.where(few, jnp.float32(3e38), (k_sel - count(lambda s: s > thr)).astype(F32)),
        lambda: jnp.zeros((1, Qb), F32))

    m_s[...] = jnp.full(m_s.shape, NEG_BIG, F32)
    acc_s[...] = jnp.zeros(acc_s.shape, F32)

    def selected(sc, seen):
        def no_ties():
            return jnp.where(sc >= thr, 1.0, 0.0), seen

        def with_ties():
            tril =(lax.broadcasted_iota(I32, (Kb, Kb), 1) <= lax.broadcasted_iota(I32, (Kb, Kb), 0))
            eq = sc == thr
            rank = seen + _dot(tril.astype(BF16), jnp.where(eq, 1.0, 0.0).astype(BF16))
            return jnp.where((sc > thr) | (eq & (rank <= need)), 1.0, 0.0), rank[Kb - 1:Kb, :]

        return lax.cond(has_ties, with_ties, no_ties)

    def attend(j, seen, diagonal):
        sel_f, seen = selected(sc_s[j], seen)
        sel = sel_f > 0.5
        kblk = kb_ref[j]
        kpos = j * Kb + krow
        if diagonal:
            adj = jnp.minimum(kpos, 2 * qpos - kpos).astype(F32)
        else:
            lane = lax.broadcasted_iota(I32, (Kb, D), 1)
            hi = (kpos >> 7).astype(F32)
            lo = (kpos & (POS_SPLIT - 1)).astype(F32)
            feat = jnp.where(lane < 3, hi, jnp.where(lane < N_POS_ROWS, lo, 0.0)).astype(BF16)
            kcat = [jnp.concatenate([kblk[:, g * D:(g + 1) * D], feat], axis=1) for g in range(N_KV_HEADS)]
        for h in range(N_ATT_HEADS):
            g = h // rpg
            if diagonal:
                lg = _dot(kblk[:, g * D:(g + 1) * D], qa_s[h, 0:D, :]) + (_alibi_slope(h) * LOG2E) * adj
            else:
                lg = _dot(kcat[g], qa_s[h])
            lg = jnp.where(sel, lg, -jnp.inf)
            lg_s[h] = lg
            m_old = m_s[h:h + 1, :]
            m_new = jnp.maximum(m_old, jnp.max(lg, axis=0, keepdims=True))
            a_s[h:h + 1, :] = jnp.exp2(m_old - m_new)
            m_s[h:h + 1, :] = m_new
        for h in range(N_ATT_HEADS):
            p_s[h] = jnp.exp2(lg_s[h] - m_s[h:h + 1, :]).astype(BF16)
        vblk = vt_ref[j]
        for h in range(N_ATT_HEADS):
            g = h // rpg
            acc_s[h] = a_s[h:h + 1, :] * acc_s[h] + _dot(vblk[g * DV:(g + 1) * DV, :], p_s[h])
        return seen

    seen = lax.fori_loop(0, i, lambda j, s: attend(j, s, False), jnp.zeros((1, Qb), F32))
    attend(i, seen, True)

    for h in range(N_ATT_HEADS):
        out_t = acc_s[h, 0:D, :] * (1.0 / acc_s[h, D:D + 1, :])
        o_ref[:, h * D:(h + 1) * D] = out_t.T


def _dsa_prompt(h2d, kb3, vt3, kib3, *, S, Qb, k_sel):
    nkb = S // Qb
    const3 = lambda shp: pl.BlockSpec(shp, lambda i: (0, 0, 0), pipeline_mode=pl.Buffered(1))
    return pl.pallas_call(
        functools.partial(_dsa_prompt_kernel, Qb=Qb, k_sel=k_sel),
        grid=(nkb,),
        in_specs=[
            pl.BlockSpec((Qb, D_ATT), lambda i: (i, OFF_Q // D_ATT)),
            pl.BlockSpec((Qb, IDX_HEADS * IDX_DIM), lambda i: (i, OFF_QI // (IDX_HEADS * IDX_DIM))),
            pl.BlockSpec((Qb, 128), lambda i: (i, OFF_SMALL // 128)),
            const3((N_ATT_HEADS, ATT_HEAD_DIM, 1)),
            const3((nkb, Qb, KV_DIM)),
            const3((nkb, N_KV_HEADS * V_AUG_ROWS, Qb)),
            const3((nkb, Qb, IDX_DIM)),
        ],
        out_specs=pl.BlockSpec((Qb, D_ATT), lambda i: (i, 0)),
        out_shape=jax.ShapeDtypeStruct((S, D_ATT), F32),
        scratch_shapes=[
            pltpu.VMEM((nkb + COUNT_UNROLL - 1, Qb, Qb), F32),
            pltpu.VMEM((N_ATT_HEADS, Qb, Qb), F32),
            pltpu.VMEM((N_ATT_HEADS, Qb, Qb), BF16),
            pltpu.VMEM((N_ATT_HEADS, 2 * ATT_HEAD_DIM, Qb), BF16),
            pltpu.VMEM((N_ATT_HEADS, Qb), F32),
            pltpu.VMEM((N_ATT_HEADS, Qb), F32),
            pltpu.VMEM((N_ATT_HEADS, V_AUG_ROWS, Qb), F32),
        ],
        compiler_params=pltpu.CompilerParams(
            dimension_semantics=("arbitrary",), vmem_limit_bytes=VMEM_LIMIT),
        name="dsa_prompt",
    )(h2d, h2d, h2d, jnp.asarray(_alibi_rows()), kb3, vt3, kib3)


def _dsa_sample_kernel(qr_ref, qir_ref, wr_ref, slope_ref, ck_ref, cv_ref, cki_ref, kn_ref, vn_ref, kin_ref,
                       o_ref, keys_s, sel_s, *, T, P, k_sel):
    LN = 128
    L = P + LN
    rpg = N_ATT_HEADS // N_KV_HEADS
    R = rpg * T

    qi_b = qir_ref[0].astype(BF16)
    w_col = wr_ref[0] * (IDX_DIM ** -0.5 * IDX_HEADS ** -0.5)

    def head_sum(s):
        return (jnp.maximum(s, 0.0) * w_col).reshape(IDX_HEADS, T, s.shape[-1]).sum(axis=0)

    sc_c = head_sum(_dot(qi_b, cki_ref[0].astype(BF16)))
    sc_n = head_sum(_dot(qi_b, kin_ref[0].astype(BF16), _NT))
    keys_s[:, 0:P] = sc_c
    col_n = lax.broadcasted_iota(I32, (T, LN), 1)
    keys_s[:, P:L] = jnp.where(col_n < T, sc_n, -jnp.inf)

    def count(pred):
        return jnp.sum(jnp.where(pred(keys_s[...]), 1, 0).astype(I32), axis=1, keepdims=True)

    thr, _, few = _kth_largest(count, k_sel, jnp.max(keys_s[...], axis=1, keepdims=True),
                               count(lambda s: s > -jnp.inf))
    need = jnp.where(few, jnp.float32(3e38), (k_sel - count(lambda s: s > thr)).astype(F32))

    triu = (lax.broadcasted_iota(I32, (LN, LN), 0) <= lax.broadcasted_iota(I32, (LN, LN), 1)).astype(BF16)
    seen = jnp.zeros((T, 1), F32)
    for jb in range(L // LN):
        kk = keys_s[:, jb * LN:(jb + 1) * LN]
        eq = kk == thr
        rank = seen + _dot(jnp.where(eq, 1.0, 0.0).astype(BF16), triu)
        sel_s[:, jb * LN:(jb + 1) * LN] = jnp.where((kk > thr) | (eq & (rank <= need)), 1.0, 0.0)
        seen = rank[:, LN - 1:LN]

    qpos = P + lax.broadcasted_iota(I32, (T, L), 0)
    kpos = lax.broadcasted_iota(I32, (T, L), 1)
    dist = jnp.abs(qpos - kpos).astype(F32)
    dist_r = jnp.concatenate([dist] * rpg, axis=0)
    sel_r = jnp.concatenate([sel_s[...]] * rpg, axis=0) > 0.5
    for g in range(N_KV_HEADS):
        sl = slice(g * ATT_HEAD_DIM, (g + 1) * ATT_HEAD_DIM)
        qg = (qr_ref[0, g * R:(g + 1) * R, :] * (ATT_HEAD_DIM ** -0.5)).astype(BF16)
        cache_rows = pl.ds(g, P, stride=N_KV_HEADS)
        lg = jnp.concatenate([_dot(qg, ck_ref[0, cache_rows, :].astype(BF16), _NT),
                              _dot(qg, kn_ref[0, :, sl].astype(BF16), _NT)], axis=1)
        lg = jnp.where(sel_r, lg - slope_ref[g * R:(g + 1) * R, :] * dist_r, -jnp.inf)
        m = jnp.max(lg, axis=1, keepdims=True)
        p = jnp.exp(lg - m)
        den = jnp.sum(p, axis=1, keepdims=True)
        pb = p.astype(BF16)
        out = (_dot(pb[:, 0:P], cv_ref[0, cache_rows, :].astype(BF16))
               + _dot(pb[:, P:L], vn_ref[0, :, sl].astype(BF16)))
        out = out * (1.0 / den)
        for r in range(rpg):
            hh = g * rpg + r
            o_ref[0, :, hh * ATT_HEAD_DIM:(hh + 1) * ATT_HEAD_DIM] = out[r * T:(r + 1) * T, :]


def _dsa_sample(q_rows, qi_rows, w_rows, slope_rows, ck, cv, cki, kn, vn, kin, *, Bd, T, P, k_sel):
    LN = 128
    per_b = lambda shp: pl.BlockSpec(shp, lambda b: (b, 0, 0))
    return pl.pallas_call(
        functools.partial(_dsa_sample_kernel, T=T, P=P, k_sel=k_sel),
        grid=(Bd,),
        in_specs=[
            per_b((1, N_ATT_HEADS * T, ATT_HEAD_DIM)),
            per_b((1, IDX_HEADS * T, IDX_DIM)),
            per_b((1, IDX_HEADS * T, 1)),
            pl.BlockSpec((N_ATT_HEADS * T, 1), lambda b: (0, 0)),
            per_b((1, N_KV_HEADS * P, ATT_HEAD_DIM)),
            per_b((1, N_KV_HEADS * P, ATT_HEAD_DIM)),
            per_b((1, IDX_DIM, P)),
            per_b((1, LN, KV_DIM)),
            per_b((1, LN, KV_DIM)),
            per_b((1, LN, IDX_DIM)),
        ],
        out_specs=per_b((1, T, D_ATT)),
        out_shape=jax.ShapeDtypeStruct((Bd, T, D_ATT), F32),
        scratch_shapes=[
            pltpu.VMEM((T, P + LN), F32),
            pltpu.VMEM((T, P + LN), F32),
        ],
        compiler_params=pltpu.CompilerParams(
            dimension_semantics=("arbitrary",), vmem_limit_bytes=VMEM_LIMIT),
        name="dsa_sample",
    )(q_rows, qi_rows, w_rows, slope_rows, ck, cv, cki, kn, vn, kin)


def _merge_kernel(x_ref, ys_ref, att_ref, za_ref, wo_ref, g_ref, b_ref, o_ref):
    att = att_ref[...] * _silu(za_ref[...])
    mix = _dot(ys_ref[...].astype(BF16), wo_ref[0:D_SSM, :]) + _dot(att.astype(BF16), wo_ref[D_SSM:, :])
    hres = ALPHA * x_ref[...] + mix
    mu = jnp.mean(hres, axis=-1, keepdims=True)
    hc = hres - mu
    var = jnp.mean(hc * hc, axis=-1, keepdims=True)
    o_ref[...] = hc * lax.rsqrt(var + LN_EPS) * g_ref[...] + b_ref[...]


def _merge(x2d, y_ssd, att, h2d, w_out_b, ln_g, ln_b, tm):
    m = x2d.shape[0]
    rows = lambda i: (i, 0)
    return pl.pallas_call(
        _merge_kernel,
        grid=(m // tm,),
        in_specs=[
            pl.BlockSpec((tm, D_MODEL), rows),
            pl.BlockSpec((tm, D_SSM), rows),
            pl.BlockSpec((tm, D_ATT), rows),
            pl.BlockSpec((tm, D_ATT), lambda i: (i, OFF_ZA // D_ATT)),
            pl.BlockSpec((D_SSM + D_ATT, D_MODEL), lambda i: (0, 0), pipeline_mode=pl.Buffered(1)),
            pl.BlockSpec((1, D_MODEL), lambda i: (0, 0)),
            pl.BlockSpec((1, D_MODEL), lambda i: (0, 0)),
        ],
        out_specs=pl.BlockSpec((tm, D_MODEL), rows),
        out_shape=jax.ShapeDtypeStruct((m, D_MODEL), F32),
        compiler_params=pltpu.CompilerParams(
            dimension_semantics=("arbitrary",), vmem_limit_bytes=VMEM_LIMIT),
        name="merge",
    )(x2d, y_ssd, att, h2d, w_out_b, ln_g, ln_b)


def _regroup_w_in(w_in):
    s = _SRC
    cols = [w_in[:, s["zs"]:s["zs"] + 1024], w_in[:, s["q"]:s["q"] + 1024], w_in[:, s["za"]:s["za"] + 1024],
            w_in[:, s["xbc"]:s["xbc"] + CONV_DIM], w_in[:, s["qi"]:s["qi"] + 512],
            w_in[:, s["k"]:s["k"] + KV_DIM], w_in[:, s["v"]:s["v"] + KV_DIM],
            w_in[:, s["dt"]:s["dt"] + 16], w_in[:, s["ki"]:s["ki"] + IDX_DIM], w_in[:, s["wi"]:s["wi"] + IDX_HEADS],
            jnp.zeros((D_MODEL, 128 - 16 - IDX_DIM - IDX_HEADS), w_in.dtype)]
    return jnp.concatenate(cols, axis=1).astype(BF16)


def _row_tile(m):
    return 256 if m % 256 == 0 else m


def kernel(x_prompt, x_sample, cache_k, cache_v, cache_kidx, state_ssm, state_conv, w_in, conv_w, conv_b,
           dt_bias, a_log, d_skip, ssd_norm_w, w_out, ln_g, ln_b):
    B, S, _ = x_prompt.shape
    Bd, T, _ = x_sample.shape
    P = cache_k.shape[1]
    assert B == 1 and S % 256 == 0 and T % 8 == 0 and T >= CONV_W - 1 and T <= 128 and P % 128 == 0
    HP = N_SSD_HEADS * SSD_HEAD_DIM

    w_pad = _regroup_w_in(w_in)
    w_out_b = w_out.astype(BF16)
    conv_b2 = conv_b.reshape(1, CONV_DIM)
    dtb2 = dt_bias.reshape(1, N_SSD_HEADS)
    alog2 = a_log.reshape(1, N_SSD_HEADS)
    dskip_full = jnp.repeat(d_skip, SSD_HEAD_DIM).reshape(1, D_SSM)
    nw2 = ssd_norm_w.reshape(1, D_SSM)
    g2 = ln_g.reshape(1, D_MODEL)
    b2 = ln_b.reshape(1, D_MODEL)

    xp = x_prompt.reshape(S, D_MODEL)
    Qb = _row_tile(S)
    h_p, k_p, v_p, ki_p, kb3, vt3, kib3 = _project(xp, w_pad, Qb)
    conv0 = jnp.zeros((1, 8, CONV_DIM), F32)
    ssm0 = jnp.zeros((1, HP, D_STATE), F32)
    yssd_p, ssm_p = _ssd(h_p, conv0, ssm0, conv_w, conv_b2, dtb2, alog2, dskip_full, nw2,
                         B=1, L=S, Q=CHUNK, Lb=256)
    att_p = _dsa_prompt(h_p, kb3, vt3, kib3, S=S, Qb=Qb, k_sel=min(TOPK_MAX, S // 4))
    y_p = _merge(xp, yssd_p, att_p, h_p, w_out_b, g2, b2, _row_tile(S))

    M = Bd * T
    xs = x_sample.reshape(M, D_MODEL)
    h_s, k_s, v_s, ki_s = _project(xs, w_pad, _row_tile(M))[:4]
    conv_prev8 = jnp.pad(state_conv, ((0, 0), (8 - (CONV_W - 1), 0), (0, 0)))
    yssd_s, ssm_s = _ssd(h_s, conv_prev8, state_ssm.reshape(Bd, HP, D_STATE), conv_w, conv_b2, dtb2, alog2,
                         dskip_full, nw2, B=Bd, L=T, Q=T, Lb=T)
    h_s3 = h_s.reshape(Bd, T, D_PAD)

    def head_rows(a, nh, dh):
        return a.reshape(Bd, T, nh, dh).transpose(0, 2, 1, 3).reshape(Bd, nh * T, dh)

    q_rows = head_rows(h_s3[:, :, OFF_Q:OFF_Q + D_ATT], N_ATT_HEADS, ATT_HEAD_DIM)
    qi_rows = head_rows(h_s3[:, :, OFF_QI:OFF_QI + IDX_HEADS * IDX_DIM], IDX_HEADS, IDX_DIM)
    w_rows = head_rows(h_s3[:, :, OFF_SMALL + SM_WI:OFF_SMALL + SM_WI + IDX_HEADS], IDX_HEADS, 1)
    slope_rows = jnp.asarray(np.repeat([_alibi_slope(h) for h in range(N_ATT_HEADS)], T).reshape(-1, 1), F32)
    padn = lambda a: jnp.pad(a.reshape(Bd, T, -1), ((0, 0), (0, 128 - T), (0, 0)))
    att_s = _dsa_sample(q_rows, qi_rows, w_rows, slope_rows,
                        cache_k.reshape(Bd, N_KV_HEADS * P, ATT_HEAD_DIM),
                        cache_v.reshape(Bd, N_KV_HEADS * P, ATT_HEAD_DIM), cache_kidx.transpose(0, 2, 1),
                        padn(k_s), padn(v_s), padn(ki_s), Bd=Bd, T=T, P=P, k_sel=min(TOPK_MAX, (P + T) // 4))
    y_s = _merge(xs, yssd_s, att_s.reshape(M, D_ATT), h_s, w_out_b, g2, b2, _row_tile(M))

    kv4 = lambda a, b_, l: a.reshape(b_, l, N_KV_HEADS, ATT_HEAD_DIM)
    st4 = lambda a, b_: a.reshape(b_, N_SSD_HEADS, SSD_HEAD_DIM, D_STATE)
    conv_p = h_p[S - (CONV_W - 1):, OFF_XBC:OFF_XBC + CONV_DIM].reshape(1, CONV_W - 1, CONV_DIM)
    conv_s = h_s3[:, T - (CONV_W - 1):, OFF_XBC:OFF_XBC + CONV_DIM]
    return (y_p.reshape(1, S, D_MODEL), y_s.reshape(Bd, T, D_MODEL),
            kv4(k_p, 1, S), kv4(v_p, 1, S), ki_p.reshape(1, S, IDX_DIM), st4(ssm_p, 1), conv_p,
            kv4(k_s, Bd, T), kv4(v_s, Bd, T), ki_s.reshape(Bd, T, IDX_DIM), st4(ssm_s, Bd), conv_s)
```

```python
import functools

import numpy as np
import jax
import jax.numpy as jnp
from jax import lax
from jax.experimental import pallas as pl
from jax.experimental.pallas import tpu as pltpu

F32 = jnp.float32
BF16 = jnp.bfloat16
I32 = jnp.int32

D_MODEL = 1024
CHUNK = 64
D_SSM = 1024
SSD_HEAD_DIM = 64
N_SSD_HEADS = 16
N_SSD_GROUPS = 2
D_STATE = 128
CONV_W = 4
CONV_DIM = D_SSM + 2 * N_SSD_GROUPS * D_STATE
D_ATT = 1024
ATT_HEAD_DIM = 128
N_ATT_HEADS = 8
N_KV_HEADS = 2
KV_DIM = N_KV_HEADS * ATT_HEAD_DIM
IDX_HEADS = 8
IDX_DIM = 64
TOPK_MAX = 256
ALPHA = 2.0 ** 0.25
LN_EPS = 1e-5
RMS_EPS = 1e-5

_SRC = dict(zs=0, xbc=1024, dt=2560, q=2576, k=3600, v=3856, za=4112, qi=5136, ki=5648, wi=5712)
OFF_ZS, OFF_Q, OFF_ZA, OFF_XBC, OFF_QI, OFF_K, OFF_V, OFF_SMALL = 0, 1024, 2048, 3072, 4608, 5120, 5376, 5632
D_PAD = 5760
SM_DT, SM_KI, SM_WI = 0, 16, 80

NEG_BIG = -1e30

VMEM_LIMIT = 56 * 1024 * 1024


def _dot(a, b, dims=(((1,), (0,)), ((), ())), precision=None):
    return lax.dot_general(a, b, dims, precision=precision, preferred_element_type=F32)


_NT = (((1,), (1,)), ((), ()))
_TN = (((0,), (0,)), ((), ()))


def _silu(x):
    return x * (1.0 / (1.0 + jnp.exp(-x)))


def _softplus(x):
    return jnp.maximum(x, 0.0) + jnp.log1p(jnp.exp(-jnp.abs(x)))


F32_LOWEST = float(np.finfo(np.float32).min)
BISECT_PASSES = 14


def _kth_largest(count, max_below, k_sel, s_min, s_max, n_valid):
    few = n_valid < k_sel
    lo = s_min
    hi = s_max + jnp.maximum(jnp.abs(s_max) * 2.0 ** -20, 1e-30)

    def halve(_, carry):
        lo, hi = carry
        mid = 0.5 * lo + 0.5 * hi
        ok = count(lambda s: s >= mid) >= k_sel
        return jnp.where(ok, mid, lo), jnp.where(ok, hi, mid)

    lo, hi = lax.fori_loop(0, BISECT_PASSES, halve, (lo, hi))

    def finish(carry):
        hi, thr, n_ge, todo, _ = carry
        v = max_below(hi)
        cnt = count(lambda s: s >= v)
        found = (todo > 0) & (cnt >= k_sel)
        thr = jnp.where(found, v, thr)
        n_ge = jnp.where(found, cnt, n_ge)
        todo = jnp.where(found, 0, todo)
        return jnp.where(todo > 0, v, hi), thr, n_ge, todo, jnp.max(todo)

    todo = jnp.where(few, 0, 1).astype(I32)
    init = (hi, jnp.full(hi.shape, F32_LOWEST, F32), jnp.full(hi.shape, k_sel, I32), todo, jnp.max(todo))
    _, thr, n_ge, _, _ = lax.while_loop(lambda c: c[4] > 0, finish, init)
    return thr, n_ge, few


def _split_bf16x3(x):
    parts, r = [], np.float32(x)
    for _ in range(3):
        p = np.float32(np.asarray(r, dtype=BF16))
        parts.append(float(p))
        r = np.float32(r - p)
    assert r == 0.0
    return parts


_PROJ_CHUNKS = tuple((c, 512) for c in range(0, 5632, 512)) + ((5632, 128),)


def _proj_kernel(x_ref, w_ref, h_ref, k_ref, v_ref, ki_ref, kb_ref, vt_ref, kib_ref):
    xb = x_ref[...].astype(BF16)
    for c0, cw in _PROJ_CHUNKS:
        h_ref[:, c0:c0 + cw] = _dot(xb, w_ref[:, c0:c0 + cw])
    k = h_ref[:, OFF_K:OFF_K + KV_DIM]
    v = h_ref[:, OFF_V:OFF_V + KV_DIM]
    ki = h_ref[:, OFF_SMALL + SM_KI:OFF_SMALL + SM_KI + IDX_DIM]
    tm = k.shape[0]
    for g in range(N_KV_HEADS):
        k_ref[pl.ds(g, tm, stride=N_KV_HEADS), :] = k[:, g * ATT_HEAD_DIM:(g + 1) * ATT_HEAD_DIM]
        v_ref[pl.ds(g, tm, stride=N_KV_HEADS), :] = v[:, g * ATT_HEAD_DIM:(g + 1) * ATT_HEAD_DIM]
    ki_ref[...] = ki
    kb_ref[0] = k.astype(BF16)
    kib_ref[0] = ki.astype(BF16)
    v_t = v.T
    for g in range(N_KV_HEADS):
        r0 = g * V_AUG_ROWS
        vt_ref[0, r0:r0 + ATT_HEAD_DIM, :] = v_t[g * ATT_HEAD_DIM:(g + 1) * ATT_HEAD_DIM, :].astype(BF16)
        vt_ref[0, r0 + ATT_HEAD_DIM:r0 + V_AUG_ROWS, :] = jnp.ones(
            (V_AUG_ROWS - ATT_HEAD_DIM, v_t.shape[1]), BF16)


def _project(x2d, w_pad, tm):
    m = x2d.shape[0]
    assert m % tm == 0
    return pl.pallas_call(
        _proj_kernel,
        grid=(m // tm,),
        in_specs=[
            pl.BlockSpec((tm, D_MODEL), lambda i: (i, 0)),
            pl.BlockSpec((D_MODEL, D_PAD), lambda i: (0, 0), pipeline_mode=pl.Buffered(1)),
        ],
        out_specs=[
            pl.BlockSpec((tm, D_PAD), lambda i: (i, 0)),
            pl.BlockSpec((N_KV_HEADS * tm, ATT_HEAD_DIM), lambda i: (i, 0)),
            pl.BlockSpec((N_KV_HEADS * tm, ATT_HEAD_DIM), lambda i: (i, 0)),
            pl.BlockSpec((tm, IDX_DIM), lambda i: (i, 0)),
            pl.BlockSpec((1, tm, KV_DIM), lambda i: (i, 0, 0)),
            pl.BlockSpec((1, N_KV_HEADS * V_AUG_ROWS, tm), lambda i: (i, 0, 0)),
            pl.BlockSpec((1, tm, IDX_DIM), lambda i: (i, 0, 0)),
        ],
        out_shape=[
            jax.ShapeDtypeStruct((m, D_PAD), F32),
            jax.ShapeDtypeStruct((N_KV_HEADS * m, ATT_HEAD_DIM), F32),
            jax.ShapeDtypeStruct((N_KV_HEADS * m, ATT_HEAD_DIM), F32),
            jax.ShapeDtypeStruct((m, IDX_DIM), F32),
            jax.ShapeDtypeStruct((m // tm, tm, KV_DIM), BF16),
            jax.ShapeDtypeStruct((m // tm, N_KV_HEADS * V_AUG_ROWS, tm), BF16),
            jax.ShapeDtypeStruct((m // tm, tm, IDX_DIM), BF16),
        ],
        compiler_params=pltpu.CompilerParams(
            dimension_semantics=("arbitrary",), vmem_limit_bytes=VMEM_LIMIT),
        name="proj",
    )(x2d, w_pad)


def _ssd_kernel(zs_ref, xbc_ref, sm_ref, cprev_ref, sprev_ref, cw_ref, cb_ref, dtb_ref, alog_ref,
                dskip_ref, nw_ref, y_ref, snew_ref, xpad, xc_s, state, *, Q, Lb):
    i = pl.program_id(1)
    nblk = pl.num_programs(1)
    H, P, N = N_SSD_HEADS, SSD_HEAD_DIM, D_STATE
    HQ = H * Q
    hp = min(H, 256 // Q)
    n_diag = H // hp
    GP = (H // N_SSD_GROUPS) * P

    @pl.when(i == 0)
    def _():
        xpad[0:8, :] = cprev_ref[0]
        state[...] = sprev_ref[0].T

    @pl.when(i > 0)
    def _():
        xpad[0:8, :] = xpad[Lb:Lb + 8, :]

    xpad[8:8 + Lb, :] = xbc_ref[...]
    conv = cb_ref[...] + xpad[5:5 + Lb, :] * cw_ref[0:1, :]
    conv = conv + xpad[6:6 + Lb, :] * cw_ref[1:2, :]
    conv = conv + xpad[7:7 + Lb, :] * cw_ref[2:3, :]
    conv = conv + xpad[8:8 + Lb, :] * cw_ref[3:4, :]
    xc_s[...] = _silu(conv)

    a_neg = -jnp.exp(alog_ref[...])

    def iota(shape, d):
        return lax.broadcasted_iota(I32, shape, d)

    lq = Q.bit_length() - 1
    e_p = (iota((H, H * P), 1) >> 6 == iota((H, H * P), 0)).astype(BF16)
    e_q = (iota((H, HQ), 1) >> lq == iota((H, HQ), 0)).astype(BF16)
    j_of = iota((Q, HQ), 1) & (Q - 1)
    t_of = iota((Q, HQ), 0)
    u_rep = (t_of <= j_of).astype(F32)
    causal = j_of <= t_of
    c6 = iota((Q, 6 * Q), 1) & (2 * Q - 1)
    r6 = iota((Q, 6 * Q), 0)
    seg_lhs3 = jnp.where(c6 < Q, (c6 <= r6).astype(F32), -1.0).astype(BF16)
    tril3 = ((iota((Q, 3 * Q), 1) & (Q - 1)) <= iota((Q, 3 * Q), 0)).astype(BF16)
    bd_mask = (iota((hp * Q, hp * P), 0) >> lq) == (iota((hp * Q, hp * P), 1) >> 6)

    def split3(x):
        x1 = x.astype(BF16)
        r1 = x - x1.astype(F32)
        x2 = r1.astype(BF16)
        return x1, x2, (r1 - x2.astype(F32)).astype(BF16)

    def expand(x, e):
        rows = x.shape[0]
        r = _dot(jnp.concatenate(split3(x), axis=0), e)
        return (r[0:rows] + r[rows:2 * rows]) + r[2 * rows:3 * rows]

    def chunk(c, carry):
        r0 = pl.multiple_of(c * Q, Q)
        dt = _softplus(sm_ref[pl.ds(r0, Q), SM_DT:SM_DT + H] + dtb_ref[...])
        a = dt * a_neg
        both = expand(jnp.concatenate([dt, a], axis=0), e_p)
        dt_full, a_full = both[:Q], both[Q:]
        acum_full = _dot(tril3, jnp.concatenate(split3(a_full), axis=0))
        tot_full = acum_full[Q - 1:Q, :]
        xs = xc_s[pl.ds(r0, Q), 0:D_SSM]
        bm = xc_s[pl.ds(r0, Q), D_SSM:D_SSM + N_SSD_GROUPS * N].astype(BF16)
        cm = xc_s[pl.ds(r0, Q), D_SSM + N_SSD_GROUPS * N:CONV_DIM].astype(BF16)
        xd = xs * dt_full
        xdw = (xd * jnp.exp(tot_full - acum_full)).astype(BF16)
        xdb = xd.astype(BF16)

        a_q = expand(a, e_q)
        seg_rhs = jnp.concatenate([a_q, a_q * u_rep], axis=0)
        seg = _dot(seg_lhs3, jnp.concatenate(split3(seg_rhs), axis=0))
        decay = jnp.exp(jnp.where(causal, seg, -jnp.inf))
        cb_parts = []
        for g in range(N_SSD_GROUPS):
            bm_g = bm[:, g * N:(g + 1) * N]
            rep = jnp.concatenate([bm_g] * (H // N_SSD_GROUPS), axis=0)
            cb_parts.append(_dot(cm[:, g * N:(g + 1) * N], rep, _NT))
        mmat = (jnp.concatenate(cb_parts, axis=1) * decay).astype(BF16)

        y_parts = []
        for d in range(n_diag):
            xd_d = xdb[:, d * hp * P:(d + 1) * hp * P]
            bd = jnp.where(bd_mask, jnp.concatenate([xd_d] * hp, axis=0), jnp.zeros((), BF16))
            y_parts.append(_dot(mmat[:, d * hp * Q:(d + 1) * hp * Q], bd))
        y_diag = jnp.concatenate(y_parts, axis=1) if n_diag > 1 else y_parts[0]

        chunk_decay = jnp.exp(tot_full)
        y_off_parts = []
        for g in range(N_SSD_GROUPS):
            st_g = state[:, g * GP:(g + 1) * GP]
            y_off_parts.append(_dot(cm[:, g * N:(g + 1) * N], st_g.astype(BF16)))
            upd = _dot(bm[:, g * N:(g + 1) * N], xdw[:, g * GP:(g + 1) * GP], _TN)
            state[:, g * GP:(g + 1) * GP] = chunk_decay[:, g * GP:(g + 1) * GP] * st_g + upd
        y_off = jnp.concatenate(y_off_parts, axis=1) * jnp.exp(acum_full)

        y = (y_diag + y_off) + dskip_ref[...] * xs
        gt = y * _silu(zs_ref[pl.ds(r0, Q), :])
        for g in range(N_SSD_GROUPS):
            gg = gt[:, g * GP:(g + 1) * GP]
            ms = jnp.mean(gg * gg, axis=-1, keepdims=True)
            y_ref[pl.ds(r0, Q), g * GP:(g + 1) * GP] = (
                gg * lax.rsqrt(ms + RMS_EPS) * nw_ref[:, g * GP:(g + 1) * GP])
        return carry

    lax.fori_loop(0, Lb // Q, chunk, 0)

    @pl.when(i == nblk - 1)
    def _():
        snew_ref[0] = state[...].T


def _ssd(h2d, conv_prev8, ssm_prev, conv_w, conv_b, dt_bias, a_log, dskip_full, norm_w, *, B, L, Q, Lb):
    assert L % Lb == 0 and Lb % Q == 0 and Lb % 8 == 0
    nblk = L // Lb
    HP = N_SSD_HEADS * SSD_HEAD_DIM
    row = lambda b, i: b * nblk + i
    full2 = lambda shp: pl.BlockSpec(shp, lambda b, i: (0, 0))
    return pl.pallas_call(
        functools.partial(_ssd_kernel, Q=Q, Lb=Lb),
        grid=(B, nblk),
        in_specs=[
            pl.BlockSpec((Lb, D_SSM), lambda b, i: (row(b, i), OFF_ZS // D_SSM)),
            pl.BlockSpec((Lb, CONV_DIM), lambda b, i: (row(b, i), OFF_XBC // CONV_DIM)),
            pl.BlockSpec((Lb, 128), lambda b, i: (row(b, i), OFF_SMALL // 128)),
            pl.BlockSpec((1, 8, CONV_DIM), lambda b, i: (b, 0, 0)),
            pl.BlockSpec((1, HP, D_STATE), lambda b, i: (b, 0, 0)),
            full2((CONV_W, CONV_DIM)),
            full2((1, CONV_DIM)),
            full2((1, N_SSD_HEADS)),
            full2((1, N_SSD_HEADS)),
            full2((1, D_SSM)),
            full2((1, D_SSM)),
        ],
        out_specs=[
            pl.BlockSpec((Lb, D_SSM), lambda b, i: (row(b, i), 0)),
            pl.BlockSpec((1, HP, D_STATE), lambda b, i: (b, 0, 0)),
        ],
        out_shape=[
            jax.ShapeDtypeStruct((B * L, D_SSM), F32),
            jax.ShapeDtypeStruct((B, HP, D_STATE), F32),
        ],
        scratch_shapes=[
            pltpu.VMEM((Lb + 8, CONV_DIM), F32),
            pltpu.VMEM((Lb, CONV_DIM), F32),
            pltpu.VMEM((D_STATE, HP), F32),
        ],
        compiler_params=pltpu.CompilerParams(
            dimension_semantics=("arbitrary", "arbitrary"), vmem_limit_bytes=VMEM_LIMIT),
        name="ssd",
    )(h2d, h2d, h2d, conv_prev8, ssm_prev, conv_w, conv_b, dt_bias, a_log, dskip_full, norm_w)


def _alibi_slope(h):
    return float(2.0 ** (-8.0 * (h + 1) / N_ATT_HEADS))


LOG2E = float(np.log2(np.e))
POS_SPLIT = 128
N_POS_ROWS = 6
V_AUG_ROWS = ATT_HEAD_DIM + 16
COUNT_UNROLL = 4


def _alibi_rows():
    rows = np.zeros((N_ATT_HEADS, ATT_HEAD_DIM, 1), np.float32)
    for h in range(N_ATT_HEADS):
        pieces = _split_bf16x3(np.float32(_alibi_slope(h) * LOG2E))
        rows[h, 0:3, 0] = [p * POS_SPLIT for p in pieces]
        rows[h, 3:6, 0] = pieces
    return rows


def _dsa_prompt_kernel(q_ref, qi_ref, sm_ref, arow_ref, kb_ref, vt_ref, kib_ref, o_ref,
                       sc_s, lg_s, p_s, qa_s, m_s, a_s, acc_s, *, Qb, k_sel):
    Kb = Qb
    i = pl.program_id(0)
    nk = i + 1
    rpg = N_ATT_HEADS // N_KV_HEADS
    D = ATT_HEAD_DIM
    DV = V_AUG_ROWS

    q_t = q_ref[...].T * (D ** -0.5 * LOG2E)
    for h in range(N_ATT_HEADS):
        qa_s[h, 0:D, :] = q_t[h * D:(h + 1) * D, :].astype(BF16)
        qa_s[h, D:2 * D, :] = jnp.broadcast_to(arow_ref[h], (D, Qb)).astype(BF16)
    qi_t = qi_ref[...].T.astype(BF16)
    w_t = sm_ref[...].T[SM_WI:SM_WI + IDX_HEADS, :] * (IDX_DIM ** -0.5 * IDX_HEADS ** -0.5)
    qpos = i * Qb + lax.broadcasted_iota(I32, (1, Qb), 1)
    chunk_end = (((qpos >> 6) + 1) << 6) - 1
    krow = lax.broadcasted_iota(I32, (Kb, 1), 0)

    @pl.when(i == 0)
    def _():
        sc_s[...] = jnp.full(sc_s.shape, -jnp.inf, F32)

    def score_blk(j, carry):
        kib = kib_ref[j]
        score = jnp.zeros((Kb, Qb), F32)
        for h in range(IDX_HEADS):
            s = _dot(kib, qi_t[h * IDX_DIM:(h + 1) * IDX_DIM, :])
            score = score + w_t[h:h + 1, :] * jnp.maximum(s, 0.0)
        score = jnp.where((j * Kb + krow) <= chunk_end, score, -jnp.inf)
        sc_s[j] = score
        s_min, s_max, n_valid = carry
        valid = score > -jnp.inf
        fold = lambda a: a.reshape(Kb // 32, 32, Qb)
        s_min = jnp.minimum(s_min, fold(jnp.where(valid, score, jnp.inf)).min(axis=0))
        s_max = jnp.maximum(s_max, fold(score).max(axis=0))
        return s_min, s_max, n_valid + fold(jnp.where(valid, 1, 0).astype(I32)).sum(axis=0)

    s_min, s_max, n_valid = lax.fori_loop(
        0, nk, score_blk,
        (jnp.full((32, Qb), jnp.inf, F32), jnp.full((32, Qb), -jnp.inf, F32), jnp.zeros((32, Qb), I32)))
    s_min = s_min.min(axis=0, keepdims=True)
    s_max = s_max.max(axis=0, keepdims=True)
    n_valid = n_valid.sum(axis=0, keepdims=True)

    trips = lax.shift_right_logical(nk + (COUNT_UNROLL - 1), COUNT_UNROLL.bit_length() - 1)

    def count(pred):
        def body(jj, acc):
            for u in range(COUNT_UNROLL):
                hit = jnp.where(pred(sc_s[jj * COUNT_UNROLL + u]), 1, 0).astype(I32)
                acc = acc + hit.reshape(Kb // 32, 32, Qb).sum(axis=0)
            return acc
        return lax.fori_loop(0, trips, body, jnp.zeros((32, Qb), I32)).sum(axis=0, keepdims=True)

    def max_below(bound):
        def body(jj, acc):
            for u in range(COUNT_UNROLL):
                sc = sc_s[jj * COUNT_UNROLL + u]
                acc = jnp.maximum(acc, jnp.where(sc < bound, sc, -jnp.inf).reshape(Kb // 32, 32, Qb).max(axis=0))
            return acc
        return lax.fori_loop(0, trips, body, jnp.full((32, Qb), -jnp.inf, F32)).max(axis=0, keepdims=True)

    thr, n_ge, few = _kth_largest(count, max_below, k_sel, s_min, s_max, n_valid)
    has_ties = jnp.max(jnp.where(n_ge > k_sel, 1, 0)) > 0
    need = lax.cond(
        has_ties,
        lambda: jnp.where(few, jnp.float32(3e38), (k_sel - count(lambda s: s > thr)).astype(F32)),
        lambda: jnp.zeros((1, Qb), F32))

    m_s[...] = jnp.full(m_s.shape, NEG_BIG, F32)
    acc_s[...] = jnp.zeros(acc_s.shape, F32)

    def selected(sc, seen):
        def no_ties():
            return jnp.where(sc >= thr, 1.0, 0.0), seen

        def with_ties():
            tril =(lax.broadcasted_iota(I32, (Kb, Kb), 1) <= lax.broadcasted_iota(I32, (Kb, Kb), 0))
            eq = sc == thr
            rank = seen + _dot(tril.astype(BF16), jnp.where(eq, 1.0, 0.0).astype(BF16))
            return jnp.where((sc > thr) | (eq & (rank <= need)), 1.0, 0.0), rank[Kb - 1:Kb, :]

        return lax.cond(has_ties, with_ties, no_ties)

    def attend(j, seen, diagonal):
        sel_f, seen = selected(sc_s[j], seen)
        sel = sel_f > 0.5
        kblk = kb_ref[j]
        kpos = j * Kb + krow
        if diagonal:
            adj = jnp.minimum(kpos, 2 * qpos - kpos).astype(F32)
        else:
            lane = lax.broadcasted_iota(I32, (Kb, D), 1)
            hi = (kpos >> 7).astype(F32)
            lo = (kpos & (POS_SPLIT - 1)).astype(F32)
            feat = jnp.where(lane < 3, hi, jnp.where(lane < N_POS_ROWS, lo, 0.0)).astype(BF16)
            kcat = [jnp.concatenate([kblk[:, g * D:(g + 1) * D], feat], axis=1) for g in range(N_KV_HEADS)]
        for h in range(N_ATT_HEADS):
            g = h // rpg
            if diagonal:
                lg = _dot(kblk[:, g * D:(g + 1) * D], qa_s[h, 0:D, :]) + (_alibi_slope(h) * LOG2E) * adj
            else:
                lg = _dot(kcat[g], qa_s[h])
            lg = jnp.where(sel, lg, -jnp.inf)
            lg_s[h] = lg
            m_old = m_s[h:h + 1, :]
            m_new = jnp.maximum(m_old, jnp.max(lg, axis=0, keepdims=True))
            a_s[h:h + 1, :] = jnp.exp2(m_old - m_new)
            m_s[h:h + 1, :] = m_new
        for h in range(N_ATT_HEADS):
            p_s[h] = jnp.exp2(lg_s[h] - m_s[h:h + 1, :]).astype(BF16)
        vblk = vt_ref[j]
        for h in range(N_ATT_HEADS):
            g = h // rpg
            acc_s[h] = a_s[h:h + 1, :] * acc_s[h] + _dot(vblk[g * DV:(g + 1) * DV, :], p_s[h])
        return seen

    seen = lax.fori_loop(0, i, lambda j, s: attend(j, s, False), jnp.zeros((1, Qb), F32))
    attend(i, seen, True)

    for h in range(N_ATT_HEADS):
        out_t = acc_s[h, 0:D, :] * (1.0 / acc_s[h, D:D + 1, :])
        o_ref[:, h * D:(h + 1) * D] = out_t.T


def _dsa_prompt(h2d, kb3, vt3, kib3, *, S, Qb, k_sel):
    nkb = S // Qb
    const3 = lambda shp: pl.BlockSpec(shp, lambda i: (0, 0, 0), pipeline_mode=pl.Buffered(1))
    return pl.pallas_call(
        functools.partial(_dsa_prompt_kernel, Qb=Qb, k_sel=k_sel),
        grid=(nkb,),
        in_specs=[
            pl.BlockSpec((Qb, D_ATT), lambda i: (i, OFF_Q // D_ATT)),
            pl.BlockSpec((Qb, IDX_HEADS * IDX_DIM), lambda i: (i, OFF_QI // (IDX_HEADS * IDX_DIM))),
            pl.BlockSpec((Qb, 128), lambda i: (i, OFF_SMALL // 128)),
            const3((N_ATT_HEADS, ATT_HEAD_DIM, 1)),
            const3((nkb, Qb, KV_DIM)),
            const3((nkb, N_KV_HEADS * V_AUG_ROWS, Qb)),
            const3((nkb, Qb, IDX_DIM)),
        ],
        out_specs=pl.BlockSpec((Qb, D_ATT), lambda i: (i, 0)),
        out_shape=jax.ShapeDtypeStruct((S, D_ATT), F32),
        scratch_shapes=[
            pltpu.VMEM((nkb + COUNT_UNROLL - 1, Qb, Qb), F32),
            pltpu.VMEM((N_ATT_HEADS, Qb, Qb), F32),
            pltpu.VMEM((N_ATT_HEADS, Qb, Qb), BF16),
            pltpu.VMEM((N_ATT_HEADS, 2 * ATT_HEAD_DIM, Qb), BF16),
            pltpu.VMEM((N_ATT_HEADS, Qb), F32),
            pltpu.VMEM((N_ATT_HEADS, Qb), F32),
            pltpu.VMEM((N_ATT_HEADS, V_AUG_ROWS, Qb), F32),
        ],
        compiler_params=pltpu.CompilerParams(
            dimension_semantics=("arbitrary",), vmem_limit_bytes=VMEM_LIMIT),
        name="dsa_prompt",
    )(h2d, h2d, h2d, jnp.asarray(_alibi_rows()), kb3, vt3, kib3)


def _dsa_sample_kernel(qr_ref, qir_ref, wr_ref, slope_ref, ck_ref, cv_ref, cki_ref, kn_ref, vn_ref, kin_ref,
                       o_ref, keys_s, sel_s, *, T, P, k_sel):
    LN = 128
    L = P + LN
    rpg = N_ATT_HEADS // N_KV_HEADS
    R = rpg * T

    qi_b = qir_ref[0].astype(BF16)
    w_col = wr_ref[0] * (IDX_DIM ** -0.5 * IDX_HEADS ** -0.5)

    def head_sum(s):
        return (jnp.maximum(s, 0.0) * w_col).reshape(IDX_HEADS, T, s.shape[-1]).sum(axis=0)

    sc_c = head_sum(_dot(qi_b, cki_ref[0].astype(BF16)))
    sc_n = head_sum(_dot(qi_b, kin_ref[0].astype(BF16), _NT))
    keys_s[:, 0:P] = sc_c
    col_n = lax.broadcasted_iota(I32, (T, LN), 1)
    keys_s[:, P:L] = jnp.where(col_n < T, sc_n, -jnp.inf)

    def count(pred):
        return jnp.sum(jnp.where(pred(keys_s[...]), 1, 0).astype(I32), axis=1, keepdims=True)

    def max_below(bound):
        sc = keys_s[...]
        return jnp.max(jnp.where(sc < bound, sc, -jnp.inf), axis=1, keepdims=True)

    sc_all = keys_s[...]
    thr, _, few = _kth_largest(
        count, max_below, k_sel,
        jnp.min(jnp.where(sc_all > -jnp.inf, sc_all, jnp.inf), axis=1, keepdims=True),
        jnp.max(sc_all, axis=1, keepdims=True), count(lambda s: s > -jnp.inf))
    need = jnp.where(few, jnp.float32(3e38), (k_sel - count(lambda s: s > thr)).astype(F32))

    triu = (lax.broadcasted_iota(I32, (LN, LN), 0) <= lax.broadcasted_iota(I32, (LN, LN), 1)).astype(BF16)
    seen = jnp.zeros((T, 1), F32)
    for jb in range(L // LN):
        kk = keys_s[:, jb * LN:(jb + 1) * LN]
        eq = kk == thr
        rank = seen + _dot(jnp.where(eq, 1.0, 0.0).astype(BF16), triu)
        sel_s[:, jb * LN:(jb + 1) * LN] = jnp.where((kk > thr) | (eq & (rank <= need)), 1.0, 0.0)
        seen = rank[:, LN - 1:LN]

    qpos = P + lax.broadcasted_iota(I32, (T, L), 0)
    kpos = lax.broadcasted_iota(I32, (T, L), 1)
    dist = jnp.abs(qpos - kpos).astype(F32)
    dist_r = jnp.concatenate([dist] * rpg, axis=0)
    sel_r = jnp.concatenate([sel_s[...]] * rpg, axis=0) > 0.5
    for g in range(N_KV_HEADS):
        sl = slice(g * ATT_HEAD_DIM, (g + 1) * ATT_HEAD_DIM)
        qg = (qr_ref[0, g * R:(g + 1) * R, :] * (ATT_HEAD_DIM ** -0.5)).astype(BF16)
        cache_rows = pl.ds(g, P, stride=N_KV_HEADS)
        lg = jnp.concatenate([_dot(qg, ck_ref[0, cache_rows, :].astype(BF16), _NT),
                              _dot(qg, kn_ref[0, :, sl].astype(BF16), _NT)], axis=1)
        lg = jnp.where(sel_r, lg - slope_ref[g * R:(g + 1) * R, :] * dist_r, -jnp.inf)
        m = jnp.max(lg, axis=1, keepdims=True)
        p = jnp.exp(lg - m)
        den = jnp.sum(p, axis=1, keepdims=True)
        pb = p.astype(BF16)
        out = (_dot(pb[:, 0:P], cv_ref[0, cache_rows, :].astype(BF16))
               + _dot(pb[:, P:L], vn_ref[0, :, sl].astype(BF16)))
        out = out * (1.0 / den)
        for r in range(rpg):
            hh = g * rpg + r
            o_ref[0, :, hh * ATT_HEAD_DIM:(hh + 1) * ATT_HEAD_DIM] = out[r * T:(r + 1) * T, :]


def _dsa_sample(q_rows, qi_rows, w_rows, slope_rows, ck, cv, cki, kn, vn, kin, *, Bd, T, P, k_sel):
    LN = 128
    per_b = lambda shp: pl.BlockSpec(shp, lambda b: (b, 0, 0))
    return pl.pallas_call(
        functools.partial(_dsa_sample_kernel, T=T, P=P, k_sel=k_sel),
        grid=(Bd,),
        in_specs=[
            per_b((1, N_ATT_HEADS * T, ATT_HEAD_DIM)),
            per_b((1, IDX_HEADS * T, IDX_DIM)),
            per_b((1, IDX_HEADS * T, 1)),
            pl.BlockSpec((N_ATT_HEADS * T, 1), lambda b: (0, 0)),
            per_b((1, N_KV_HEADS * P, ATT_HEAD_DIM)),
            per_b((1, N_KV_HEADS * P, ATT_HEAD_DIM)),
            per_b((1, IDX_DIM, P)),
            per_b((1, LN, KV_DIM)),
            per_b((1, LN, KV_DIM)),
            per_b((1, LN, IDX_DIM)),
        ],
        out_specs=per_b((1, T, D_ATT)),
        out_shape=jax.ShapeDtypeStruct((Bd, T, D_ATT), F32),
        scratch_shapes=[
            pltpu.VMEM((T, P + LN), F32),
            pltpu.VMEM((T, P + LN), F32),
        ],
        compiler_params=pltpu.CompilerParams(
            dimension_semantics=("arbitrary",), vmem_limit_bytes=VMEM_LIMIT),
        name="dsa_sample",
    )(q_rows, qi_rows, w_rows, slope_rows, ck, cv, cki, kn, vn, kin)


def _merge_kernel(x_ref, ys_ref, att_ref, za_ref, wo_ref, g_ref, b_ref, o_ref):
    att = att_ref[...] * _silu(za_ref[...])
    mix = _dot(ys_ref[...].astype(BF16), wo_ref[0:D_SSM, :]) + _dot(att.astype(BF16), wo_ref[D_SSM:, :])
    hres = ALPHA * x_ref[...] + mix
    mu = jnp.mean(hres, axis=-1, keepdims=True)
    hc = hres - mu
    var = jnp.mean(hc * hc, axis=-1, keepdims=True)
    o_ref[...] = hc * lax.rsqrt(var + LN_EPS) * g_ref[...] + b_ref[...]


def _merge(x2d, y_ssd, att, h2d, w_out_b, ln_g, ln_b, tm):
    m = x2d.shape[0]
    rows = lambda i: (i, 0)
    return pl.pallas_call(
        _merge_kernel,
        grid=(m // tm,),
        in_specs=[
            pl.BlockSpec((tm, D_MODEL), rows),
            pl.BlockSpec((tm, D_SSM), rows),
            pl.BlockSpec((tm, D_ATT), rows),
            pl.BlockSpec((tm, D_ATT), lambda i: (i, OFF_ZA // D_ATT)),
            pl.BlockSpec((D_SSM + D_ATT, D_MODEL), lambda i: (0, 0), pipeline_mode=pl.Buffered(1)),
            pl.BlockSpec((1, D_MODEL), lambda i: (0, 0)),
            pl.BlockSpec((1, D_MODEL), lambda i: (0, 0)),
        ],
        out_specs=pl.BlockSpec((tm, D_MODEL), rows),
        out_shape=jax.ShapeDtypeStruct((m, D_MODEL), F32),
        compiler_params=pltpu.CompilerParams(
            dimension_semantics=("arbitrary",), vmem_limit_bytes=VMEM_LIMIT),
        name="merge",
    )(x2d, y_ssd, att, h2d, w_out_b, ln_g, ln_b)


def _regroup_w_in(w_in):
    s = _SRC
    cols = [w_in[:, s["zs"]:s["zs"] + 1024], w_in[:, s["q"]:s["q"] + 1024], w_in[:, s["za"]:s["za"] + 1024],
            w_in[:, s["xbc"]:s["xbc"] + CONV_DIM], w_in[:, s["qi"]:s["qi"] + 512],
            w_in[:, s["k"]:s["k"] + KV_DIM], w_in[:, s["v"]:s["v"] + KV_DIM],
            w_in[:, s["dt"]:s["dt"] + 16], w_in[:, s["ki"]:s["ki"] + IDX_DIM], w_in[:, s["wi"]:s["wi"] + IDX_HEADS],
            jnp.zeros((D_MODEL, 128 - 16 - IDX_DIM - IDX_HEADS), w_in.dtype)]
    return jnp.concatenate(cols, axis=1).astype(BF16)


def _row_tile(m):
    return 256 if m % 256 == 0 else m


def kernel(x_prompt, x_sample, cache_k, cache_v, cache_kidx, state_ssm, state_conv, w_in, conv_w, conv_b,
           dt_bias, a_log, d_skip, ssd_norm_w, w_out, ln_g, ln_b):
    B, S, _ = x_prompt.shape
    Bd, T, _ = x_sample.shape
    P = cache_k.shape[1]
    assert B == 1 and S % 256 == 0 and T % 8 == 0 and T >= CONV_W - 1 and T <= 128 and P % 128 == 0
    HP = N_SSD_HEADS * SSD_HEAD_DIM

    w_pad = _regroup_w_in(w_in)
    w_out_b = w_out.astype(BF16)
    conv_b2 = conv_b.reshape(1, CONV_DIM)
    dtb2 = dt_bias.reshape(1, N_SSD_HEADS)
    alog2 = a_log.reshape(1, N_SSD_HEADS)
    dskip_full = jnp.repeat(d_skip, SSD_HEAD_DIM).reshape(1, D_SSM)
    nw2 = ssd_norm_w.reshape(1, D_SSM)
    g2 = ln_g.reshape(1, D_MODEL)
    b2 = ln_b.reshape(1, D_MODEL)

    xp = x_prompt.reshape(S, D_MODEL)
    Qb = _row_tile(S)
    h_p, k_p, v_p, ki_p, kb3, vt3, kib3 = _project(xp, w_pad, Qb)
    conv0 = jnp.zeros((1, 8, CONV_DIM), F32)
    ssm0 = jnp.zeros((1, HP, D_STATE), F32)
    yssd_p, ssm_p = _ssd(h_p, conv0, ssm0, conv_w, conv_b2, dtb2, alog2, dskip_full, nw2,
                         B=1, L=S, Q=CHUNK, Lb=256)
    att_p = _dsa_prompt(h_p, kb3, vt3, kib3, S=S, Qb=Qb, k_sel=min(TOPK_MAX, S // 4))
    y_p = _merge(xp, yssd_p, att_p, h_p, w_out_b, g2, b2, _row_tile(S))

    M = Bd * T
    xs = x_sample.reshape(M, D_MODEL)
    h_s, k_s, v_s, ki_s = _project(xs, w_pad, _row_tile(M))[:4]
    conv_prev8 = jnp.pad(state_conv, ((0, 0), (8 - (CONV_W - 1), 0), (0, 0)))
    yssd_s, ssm_s = _ssd(h_s, conv_prev8, state_ssm.reshape(Bd, HP, D_STATE), conv_w, conv_b2, dtb2, alog2,
                         dskip_full, nw2, B=Bd, L=T, Q=T, Lb=T)
    h_s3 = h_s.reshape(Bd, T, D_PAD)

    def head_rows(a, nh, dh):
        return a.reshape(Bd, T, nh, dh).transpose(0, 2, 1, 3).reshape(Bd, nh * T, dh)

    q_rows = head_rows(h_s3[:, :, OFF_Q:OFF_Q + D_ATT], N_ATT_HEADS, ATT_HEAD_DIM)
    qi_rows = head_rows(h_s3[:, :, OFF_QI:OFF_QI + IDX_HEADS * IDX_DIM], IDX_HEADS, IDX_DIM)
    w_rows = head_rows(h_s3[:, :, OFF_SMALL + SM_WI:OFF_SMALL + SM_WI + IDX_HEADS], IDX_HEADS, 1)
    slope_rows = jnp.asarray(np.repeat([_alibi_slope(h) for h in range(N_ATT_HEADS)], T).reshape(-1, 1), F32)
    padn = lambda a: jnp.pad(a.reshape(Bd, T, -1), ((0, 0), (0, 128 - T), (0, 0)))
    att_s = _dsa_sample(q_rows, qi_rows, w_rows, slope_rows,
                        cache_k.reshape(Bd, N_KV_HEADS * P, ATT_HEAD_DIM),
                        cache_v.reshape(Bd, N_KV_HEADS * P, ATT_HEAD_DIM), cache_kidx.transpose(0, 2, 1),
                        padn(k_s), padn(v_s), padn(ki_s), Bd=Bd, T=T, P=P, k_sel=min(TOPK_MAX, (P + T) // 4))
    y_s = _merge(xs, yssd_s, att_s.reshape(M, D_ATT), h_s, w_out_b, g2, b2, _row_tile(M))

    kv4 = lambda a, b_, l: a.reshape(b_, l, N_KV_HEADS, ATT_HEAD_DIM)
    st4 = lambda a, b_: a.reshape(b_, N_SSD_HEADS, SSD_HEAD_DIM, D_STATE)
    conv_p = h_p[S - (CONV_W - 1):, OFF_XBC:OFF_XBC + CONV_DIM].reshape(1, CONV_W - 1, CONV_DIM)
    conv_s = h_s3[:, T - (CONV_W - 1):, OFF_XBC:OFF_XBC + CONV_DIM]
    return (y_p.reshape(1, S, D_MODEL), y_s.reshape(Bd, T, D_MODEL),
            kv4(k_p, 1, S), kv4(v_p, 1, S), ki_p.reshape(1, S, IDX_DIM), st4(ssm_p, 1), conv_p,
            kv4(k_s, Bd, T), kv4(v_s, Bd, T), ki_s.reshape(Bd, T, IDX_DIM), st4(ssm_s, Bd), conv_s)
```

```python
import functools

import numpy as np
import jax
import jax.numpy as jnp
from jax import lax
from jax.experimental import pallas as pl
from jax.experimental.pallas import tpu as pltpu

F32 = jnp.float32
BF16 = jnp.bfloat16
I32 = jnp.int32

D_MODEL = 1024
CHUNK = 64
D_SSM = 1024
SSD_HEAD_DIM = 64
N_SSD_HEADS = 16
N_SSD_GROUPS = 2
D_STATE = 128
CONV_W = 4
CONV_DIM = D_SSM + 2 * N_SSD_GROUPS * D_STATE
D_ATT = 1024
ATT_HEAD_DIM = 128
N_ATT_HEADS = 8
N_KV_HEADS = 2
KV_DIM = N_KV_HEADS * ATT_HEAD_DIM
IDX_HEADS = 8
IDX_DIM = 64
TOPK_MAX = 256
ALPHA = 2.0 ** 0.25
LN_EPS = 1e-5
RMS_EPS = 1e-5

_SRC = dict(zs=0, xbc=1024, dt=2560, q=2576, k=3600, v=3856, za=4112, qi=5136, ki=5648, wi=5712)
OFF_ZS, OFF_Q, OFF_ZA, OFF_XBC, OFF_QI, OFF_K, OFF_V, OFF_SMALL = 0, 1024, 2048, 3072, 4608, 5120, 5376, 5632
D_PAD = 5760
SM_DT, SM_KI, SM_WI = 0, 16, 80

NEG_BIG = -1e30

VMEM_LIMIT = 56 * 1024 * 1024


def _dot(a, b, dims=(((1,), (0,)), ((), ())), precision=None):
    return lax.dot_general(a, b, dims, precision=precision, preferred_element_type=F32)


_NT = (((1,), (1,)), ((), ()))
_TN = (((0,), (0,)), ((), ()))


def _silu(x):
    return x * (1.0 / (1.0 + jnp.exp(-x)))


def _softplus(x):
    return jnp.maximum(x, 0.0) + jnp.log1p(jnp.exp(-jnp.abs(x)))


F32_LOWEST = float(np.finfo(np.float32).min)
BISECT_PASSES = 14


def _kth_largest(count, max_below, k_sel, s_min, s_max, n_valid):
    few = n_valid < k_sel
    lo = s_min
    hi = s_max + jnp.maximum(jnp.abs(s_max) * 2.0 ** -20, 1e-30)

    def halve(_, carry):
        lo, hi = carry
        mid = 0.5 * lo + 0.5 * hi
        ok = count(lambda s: s >= mid) >= k_sel
        return jnp.where(ok, mid, lo), jnp.where(ok, hi, mid)

    lo, hi = lax.fori_loop(0, BISECT_PASSES, halve, (lo, hi))

    def finish(carry):
        hi, thr, n_ge, todo, _ = carry
        v = max_below(hi)
        cnt = count(lambda s: s >= v)
        found = (todo > 0) & (cnt >= k_sel)
        thr = jnp.where(found, v, thr)
        n_ge = jnp.where(found, cnt, n_ge)
        todo = jnp.where(found, 0, todo)
        return jnp.where(todo > 0, v, hi), thr, n_ge, todo, jnp.max(todo)

    todo = jnp.where(few, 0, 1).astype(I32)
    init = (hi, jnp.full(hi.shape, F32_LOWEST, F32), jnp.full(hi.shape, k_sel, I32), todo, jnp.max(todo))
    _, thr, n_ge, _, _ = lax.while_loop(lambda c: c[4] > 0, finish, init)
    return thr, n_ge, few


def _split_bf16x3(x):
    parts, r = [], np.float32(x)
    for _ in range(3):
        p = np.float32(np.asarray(r, dtype=BF16))
        parts.append(float(p))
        r = np.float32(r - p)
    assert r == 0.0
    return parts


_PROJ_CHUNKS = tuple((c, 512) for c in range(0, 5632, 512)) + ((5632, 128),)


def _proj_kernel(x_ref, w_ref, h_ref, k_ref, v_ref, ki_ref, kb_ref, vt_ref, kib_ref):
    xb = x_ref[...].astype(BF16)
    for c0, cw in _PROJ_CHUNKS:
        h_ref[:, c0:c0 + cw] = _dot(xb, w_ref[:, c0:c0 + cw])
    k = h_ref[:, OFF_K:OFF_K + KV_DIM]
    v = h_ref[:, OFF_V:OFF_V + KV_DIM]
    ki = h_ref[:, OFF_SMALL + SM_KI:OFF_SMALL + SM_KI + IDX_DIM]
    tm = k.shape[0]
    for g in range(N_KV_HEADS):
        k_ref[pl.ds(g, tm, stride=N_KV_HEADS), :] = k[:, g * ATT_HEAD_DIM:(g + 1) * ATT_HEAD_DIM]
        v_ref[pl.ds(g, tm, stride=N_KV_HEADS), :] = v[:, g * ATT_HEAD_DIM:(g + 1) * ATT_HEAD_DIM]
    ki_ref[...] = ki
    kb_ref[0] = k.astype(BF16)
    kib_ref[0] = ki.astype(BF16)
    v_t = v.T
    for g in range(N_KV_HEADS):
        r0 = g * V_AUG_ROWS
        vt_ref[0, r0:r0 + ATT_HEAD_DIM, :] = v_t[g * ATT_HEAD_DIM:(g + 1) * ATT_HEAD_DIM, :].astype(BF16)
        vt_ref[0, r0 + ATT_HEAD_DIM:r0 + V_AUG_ROWS, :] = jnp.ones(
            (V_AUG_ROWS - ATT_HEAD_DIM, v_t.shape[1]), BF16)


def _project(x2d, w_pad, tm):
    m = x2d.shape[0]
    assert m % tm == 0
    return pl.pallas_call(
        _proj_kernel,
        grid=(m // tm,),
        in_specs=[
            pl.BlockSpec((tm, D_MODEL), lambda i: (i, 0)),
            pl.BlockSpec((D_MODEL, D_PAD), lambda i: (0, 0), pipeline_mode=pl.Buffered(1)),
        ],
        out_specs=[
            pl.BlockSpec((tm, D_PAD), lambda i: (i, 0)),
            pl.BlockSpec((N_KV_HEADS * tm, ATT_HEAD_DIM), lambda i: (i, 0)),
            pl.BlockSpec((N_KV_HEADS * tm, ATT_HEAD_DIM), lambda i: (i, 0)),
            pl.BlockSpec((tm, IDX_DIM), lambda i: (i, 0)),
            pl.BlockSpec((1, tm, KV_DIM), lambda i: (i, 0, 0)),
            pl.BlockSpec((1, N_KV_HEADS * V_AUG_ROWS, tm), lambda i: (i, 0, 0)),
            pl.BlockSpec((1, tm, IDX_DIM), lambda i: (i, 0, 0)),
        ],
        out_shape=[
            jax.ShapeDtypeStruct((m, D_PAD), F32),
            jax.ShapeDtypeStruct((N_KV_HEADS * m, ATT_HEAD_DIM), F32),
            jax.ShapeDtypeStruct((N_KV_HEADS * m, ATT_HEAD_DIM), F32),
            jax.ShapeDtypeStruct((m, IDX_DIM), F32),
            jax.ShapeDtypeStruct((m // tm, tm, KV_DIM), BF16),
            jax.ShapeDtypeStruct((m // tm, N_KV_HEADS * V_AUG_ROWS, tm), BF16),
            jax.ShapeDtypeStruct((m // tm, tm, IDX_DIM), BF16),
        ],
        compiler_params=pltpu.CompilerParams(
            dimension_semantics=("arbitrary",), vmem_limit_bytes=VMEM_LIMIT),
        name="proj",
    )(x2d, w_pad)


def _ssd_kernel(zs_ref, xbc_ref, sm_ref, cprev_ref, sprev_ref, cw_ref, cb_ref, dtb_ref, alog_ref,
                dskip_ref, nw_ref, y_ref, snew_ref, xpad, xc_s, state, *, Q, Lb):
    i = pl.program_id(1)
    nblk = pl.num_programs(1)
    H, P, N = N_SSD_HEADS, SSD_HEAD_DIM, D_STATE
    HQ = H * Q
    hp = min(H, 256 // Q)
    n_diag = H // hp
    GP = (H // N_SSD_GROUPS) * P

    @pl.when(i == 0)
    def _():
        xpad[0:8, :] = cprev_ref[0]
        state[...] = sprev_ref[0].T

    @pl.when(i > 0)
    def _():
        xpad[0:8, :] = xpad[Lb:Lb + 8, :]

    xpad[8:8 + Lb, :] = xbc_ref[...]
    conv = cb_ref[...] + xpad[5:5 + Lb, :] * cw_ref[0:1, :]
    conv = conv + xpad[6:6 + Lb, :] * cw_ref[1:2, :]
    conv = conv + xpad[7:7 + Lb, :] * cw_ref[2:3, :]
    conv = conv + xpad[8:8 + Lb, :] * cw_ref[3:4, :]
    xc_s[...] = _silu(conv)

    a_neg = -jnp.exp(alog_ref[...])

    def iota(shape, d):
        return lax.broadcasted_iota(I32, shape, d)

    lq = Q.bit_length() - 1
    e_p = (iota((H, H * P), 1) >> 6 == iota((H, H * P), 0)).astype(BF16)
    e_q = (iota((H, HQ), 1) >> lq == iota((H, HQ), 0)).astype(BF16)
    j_of = iota((Q, HQ), 1) & (Q - 1)
    t_of = iota((Q, HQ), 0)
    u_rep = (t_of <= j_of).astype(F32)
    causal = j_of <= t_of
    c6 = iota((Q, 6 * Q), 1) & (2 * Q - 1)
    r6 = iota((Q, 6 * Q), 0)
    seg_lhs3 = jnp.where(c6 < Q, (c6 <= r6).astype(F32), -1.0).astype(BF16)
    tril3 = ((iota((Q, 3 * Q), 1) & (Q - 1)) <= iota((Q, 3 * Q), 0)).astype(BF16)
    bd_mask = (iota((hp * Q, hp * P), 0) >> lq) == (iota((hp * Q, hp * P), 1) >> 6)

    def split3(x):
        x1 = x.astype(BF16)
        r1 = x - x1.astype(F32)
        x2 = r1.astype(BF16)
        return x1, x2, (r1 - x2.astype(F32)).astype(BF16)

    def expand(x, e):
        rows = x.shape[0]
        r = _dot(jnp.concatenate(split3(x), axis=0), e)
        return (r[0:rows] + r[rows:2 * rows]) + r[2 * rows:3 * rows]

    def chunk(c, carry):
        r0 = pl.multiple_of(c * Q, Q)
        dt = _softplus(sm_ref[pl.ds(r0, Q), SM_DT:SM_DT + H] + dtb_ref[...])
        a = dt * a_neg
        both = expand(jnp.concatenate([dt, a], axis=0), e_p)
        dt_full, a_full = both[:Q], both[Q:]
        acum_full = _dot(tril3, jnp.concatenate(split3(a_full), axis=0))
        tot_full = acum_full[Q - 1:Q, :]
        xs = xc_s[pl.ds(r0, Q), 0:D_SSM]
        bm = xc_s[pl.ds(r0, Q), D_SSM:D_SSM + N_SSD_GROUPS * N].astype(BF16)
        cm = xc_s[pl.ds(r0, Q), D_SSM + N_SSD_GROUPS * N:CONV_DIM].astype(BF16)
        xd = xs * dt_full
        xdw = (xd * jnp.exp(tot_full - acum_full)).astype(BF16)
        xdb = xd.astype(BF16)

        a_q = expand(a, e_q)
        seg_rhs = jnp.concatenate([a_q, a_q * u_rep], axis=0)
        seg = _dot(seg_lhs3, jnp.concatenate(split3(seg_rhs), axis=0))
        decay = jnp.exp(jnp.where(causal, seg, -jnp.inf))
        cb_parts = []
        for g in range(N_SSD_GROUPS):
            bm_g = bm[:, g * N:(g + 1) * N]
            rep = jnp.concatenate([bm_g] * (H // N_SSD_GROUPS), axis=0)
            cb_parts.append(_dot(cm[:, g * N:(g + 1) * N], rep, _NT))
        mmat = (jnp.concatenate(cb_parts, axis=1) * decay).astype(BF16)

        y_parts = []
        for d in range(n_diag):
            xd_d = xdb[:, d * hp * P:(d + 1) * hp * P]
            bd = jnp.where(bd_mask, jnp.concatenate([xd_d] * hp, axis=0), jnp.zeros((), BF16))
            y_parts.append(_dot(mmat[:, d * hp * Q:(d + 1) * hp * Q], bd))
        y_diag = jnp.concatenate(y_parts, axis=1) if n_diag > 1 else y_parts[0]

        chunk_decay = jnp.exp(tot_full)
        y_off_parts = []
        for g in range(N_SSD_GROUPS):
            st_g = state[:, g * GP:(g + 1) * GP]
            y_off_parts.append(_dot(cm[:, g * N:(g + 1) * N], st_g.astype(BF16)))
            upd = _dot(bm[:, g * N:(g + 1) * N], xdw[:, g * GP:(g + 1) * GP], _TN)
            state[:, g * GP:(g + 1) * GP] = chunk_decay[:, g * GP:(g + 1) * GP] * st_g + upd
        y_off = jnp.concatenate(y_off_parts, axis=1) * jnp.exp(acum_full)

        y = (y_diag + y_off) + dskip_ref[...] * xs
        gt = y * _silu(zs_ref[pl.ds(r0, Q), :])
        for g in range(N_SSD_GROUPS):
            gg = gt[:, g * GP:(g + 1) * GP]
            ms = jnp.mean(gg * gg, axis=-1, keepdims=True)
            y_ref[pl.ds(r0, Q), g * GP:(g + 1) * GP] = (
                gg * lax.rsqrt(ms + RMS_EPS) * nw_ref[:, g * GP:(g + 1) * GP])
        return carry

    lax.fori_loop(0, Lb // Q, chunk, 0)

    @pl.when(i == nblk - 1)
    def _():
        snew_ref[0] = state[...].T


def _ssd(h2d, conv_prev8, ssm_prev, conv_w, conv_b, dt_bias, a_log, dskip_full, norm_w, *, B, L, Q, Lb):
    assert L % Lb == 0 and Lb % Q == 0 and Lb % 8 == 0
    nblk = L // Lb
    HP = N_SSD_HEADS * SSD_HEAD_DIM
    row = lambda b, i: b * nblk + i
    full2 = lambda shp: pl.BlockSpec(shp, lambda b, i: (0, 0))
    return pl.pallas_call(
        functools.partial(_ssd_kernel, Q=Q, Lb=Lb),
        grid=(B, nblk),
        in_specs=[
            pl.BlockSpec((Lb, D_SSM), lambda b, i: (row(b, i), OFF_ZS // D_SSM)),
            pl.BlockSpec((Lb, CONV_DIM), lambda b, i: (row(b, i), OFF_XBC // CONV_DIM)),
            pl.BlockSpec((Lb, 128), lambda b, i: (row(b, i), OFF_SMALL // 128)),
            pl.BlockSpec((1, 8, CONV_DIM), lambda b, i: (b, 0, 0)),
            pl.BlockSpec((1, HP, D_STATE), lambda b, i: (b, 0, 0)),
            full2((CONV_W, CONV_DIM)),
            full2((1, CONV_DIM)),
            full2((1, N_SSD_HEADS)),
            full2((1, N_SSD_HEADS)),
            full2((1, D_SSM)),
            full2((1, D_SSM)),
        ],
        out_specs=[
            pl.BlockSpec((Lb, D_SSM), lambda b, i: (row(b, i), 0)),
            pl.BlockSpec((1, HP, D_STATE), lambda b, i: (b, 0, 0)),
        ],
        out_shape=[
            jax.ShapeDtypeStruct((B * L, D_SSM), F32),
            jax.ShapeDtypeStruct((B, HP, D_STATE), F32),
        ],
        scratch_shapes=[
            pltpu.VMEM((Lb + 8, CONV_DIM), F32),
            pltpu.VMEM((Lb, CONV_DIM), F32),
            pltpu.VMEM((D_STATE, HP), F32),
        ],
        compiler_params=pltpu.CompilerParams(
            dimension_semantics=("arbitrary", "arbitrary"), vmem_limit_bytes=VMEM_LIMIT),
        name="ssd",
    )(h2d, h2d, h2d, conv_prev8, ssm_prev, conv_w, conv_b, dt_bias, a_log, dskip_full, norm_w)


def _alibi_slope(h):
    return float(2.0 ** (-8.0 * (h + 1) / N_ATT_HEADS))


LOG2E = float(np.log2(np.e))
POS_SPLIT = 128
N_POS_ROWS = 6
V_AUG_ROWS = ATT_HEAD_DIM + 16
COUNT_UNROLL = 4


def _alibi_rows():
    rows = np.zeros((N_ATT_HEADS, ATT_HEAD_DIM, 1), np.float32)
    for h in range(N_ATT_HEADS):
        pieces = _split_bf16x3(np.float32(_alibi_slope(h) * LOG2E))
        rows[h, 0:3, 0] = [p * POS_SPLIT for p in pieces]
        rows[h, 3:6, 0] = pieces
    return rows


def _dsa_prompt_kernel(q_ref, qi_ref, sm_ref, arow_ref, kb_ref, vt_ref, kib_ref, o_ref,
                       sc_s, lg_s, p_s, qa_s, m_s, a_s, acc_s, *, Qb, k_sel):
    Kb = Qb
    i = pl.program_id(0)
    nk = i + 1
    rpg = N_ATT_HEADS // N_KV_HEADS
    D = ATT_HEAD_DIM
    DV = V_AUG_ROWS

    q_t = q_ref[...].T * (D ** -0.5 * LOG2E)
    for h in range(N_ATT_HEADS):
        qa_s[h, 0:D, :] = q_t[h * D:(h + 1) * D, :].astype(BF16)
        qa_s[h, D:2 * D, :] = jnp.broadcast_to(arow_ref[h], (D, Qb)).astype(BF16)
    qi_t = qi_ref[...].T.astype(BF16)
    w_t = sm_ref[...].T[SM_WI:SM_WI + IDX_HEADS, :] * (IDX_DIM ** -0.5 * IDX_HEADS ** -0.5)
    qpos = i * Qb + lax.broadcasted_iota(I32, (1, Qb), 1)
    chunk_end = (((qpos >> 6) + 1) << 6) - 1
    krow = lax.broadcasted_iota(I32, (Kb, 1), 0)

    @pl.when(i == 0)
    def _():
        sc_s[...] = jnp.full(sc_s.shape, -jnp.inf, F32)

    def score_blk(j, carry):
        kib = kib_ref[j]
        score = jnp.zeros((Kb, Qb), F32)
        for h in range(IDX_HEADS):
            s = _dot(kib, qi_t[h * IDX_DIM:(h + 1) * IDX_DIM, :])
            score = score + w_t[h:h + 1, :] * jnp.maximum(s, 0.0)
        score = jnp.where((j * Kb + krow) <= chunk_end, score, -jnp.inf)
        sc_s[j] = score
        s_min, s_max, n_valid = carry
        valid = score > -jnp.inf
        fold = lambda a: a.reshape(Kb // 32, 32, Qb)
        s_min = jnp.minimum(s_min, fold(jnp.where(valid, score, jnp.inf)).min(axis=0))
        s_max = jnp.maximum(s_max, fold(score).max(axis=0))
        return s_min, s_max, n_valid + fold(jnp.where(valid, 1, 0).astype(I32)).sum(axis=0)

    s_min, s_max, n_valid = lax.fori_loop(
        0, nk, score_blk,
        (jnp.full((32, Qb), jnp.inf, F32), jnp.full((32, Qb), -jnp.inf, F32), jnp.zeros((32, Qb), I32)))
    s_min = s_min.min(axis=0, keepdims=True)
    s_max = s_max.max(axis=0, keepdims=True)
    n_valid = n_valid.sum(axis=0, keepdims=True)

    trips = lax.shift_right_logical(nk + (COUNT_UNROLL - 1), COUNT_UNROLL.bit_length() - 1)

    def count(pred):
        def body(jj, acc):
            for u in range(COUNT_UNROLL):
                hit = jnp.where(pred(sc_s[jj * COUNT_UNROLL + u]), 1, 0).astype(I32)
                acc = acc + hit.reshape(Kb // 32, 32, Qb).sum(axis=0)
            return acc
        return lax.fori_loop(0, trips, body, jnp.zeros((32, Qb), I32)).sum(axis=0, keepdims=True)

    def max_below(bound):
        def body(jj, acc):
            for u in range(COUNT_UNROLL):
                sc = sc_s[jj * COUNT_UNROLL + u]
                acc = jnp.maximum(acc, jnp.where(sc < bound, sc, -jnp.inf).reshape(Kb // 32, 32, Qb).max(axis=0))
            return acc
        return lax.fori_loop(0, trips, body, jnp.full((32, Qb), -jnp.inf, F32)).max(axis=0, keepdims=True)

    thr, n_ge, few = _kth_largest(count, max_below, k_sel, s_min, s_max, n_valid)
    tied = n_ge > k_sel
    NO_LIMIT = 2 ** 30

    def last_tie_positions():
        need = (k_sel - count(lambda s: s > thr)).astype(F32)
        tril = (lax.broadcasted_iota(I32, (Kb, Kb), 1) <= lax.broadcasted_iota(I32, (Kb, Kb), 0)).astype(BF16)

        def blk(j, carry):
            seen, pos = carry
            eq = jnp.where(sc_s[j] == thr, 1.0, 0.0)
            here = jnp.sum(eq, axis=0, keepdims=True)
            crossing = tied & (seen < need) & (seen + here >= need)

            def locate():
                rank = seen + _dot(tril, eq.astype(BF16))
                before = jnp.sum(jnp.where(rank < need, 1, 0).astype(I32), axis=0, keepdims=True)
                return j * Kb + before

            pos_j = lax.cond(jnp.max(jnp.where(crossing, 1, 0)) > 0, locate, lambda: jnp.zeros((1, Qb), I32))
            return seen + here, jnp.where(crossing, pos_j, pos)

        init = (jnp.zeros((1, Qb), F32), jnp.full((1, Qb), NO_LIMIT, I32))
        return lax.fori_loop(0, nk, blk, init)[1]

    tie_pos = lax.cond(jnp.max(jnp.where(tied, 1, 0)) > 0, last_tie_positions,
                       lambda: jnp.full((1, Qb), NO_LIMIT, I32))

    m_s[...] = jnp.full(m_s.shape, NEG_BIG, F32)
    acc_s[...] = jnp.zeros(acc_s.shape, F32)

    def attend(j, diagonal):
        sc = sc_s[j]
        kblk = kb_ref[j]
        kpos = j * Kb + krow
        bias = jnp.where(sc > thr, 0.0,
                         jnp.where(sc == thr, jnp.where(kpos <= tie_pos, 0.0, -jnp.inf), -jnp.inf))
        if diagonal:
            adj = jnp.minimum(kpos, 2 * qpos - kpos).astype(F32)
        else:
            lane = lax.broadcasted_iota(I32, (Kb, D), 1)
            hi = (kpos >> 7).astype(F32)
            lo = (kpos & (POS_SPLIT - 1)).astype(F32)
            feat = jnp.where(lane < 3, hi, jnp.where(lane < N_POS_ROWS, lo, 0.0)).astype(BF16)
            kcat = [jnp.concatenate([kblk[:, g * D:(g + 1) * D], feat], axis=1) for g in range(N_KV_HEADS)]
        for h in range(N_ATT_HEADS):
            g = h // rpg
            if diagonal:
                lg = _dot(kblk[:, g * D:(g + 1) * D], qa_s[h, 0:D, :]) + (_alibi_slope(h) * LOG2E) * adj
            else:
                lg = _dot(kcat[g], qa_s[h])
            lg = lg + bias
            lg_s[h] = lg
            m_old = m_s[h:h + 1, :]
            m_new = jnp.maximum(m_old, jnp.max(lg, axis=0, keepdims=True))
            a_s[h:h + 1, :] = jnp.exp2(m_old - m_new)
            m_s[h:h + 1, :] = m_new
        for h in range(N_ATT_HEADS):
            p_s[h] = jnp.exp2(lg_s[h] - m_s[h:h + 1, :]).astype(BF16)
        vblk = vt_ref[j]
        for h in range(N_ATT_HEADS):
            g = h // rpg
            acc_s[h] = a_s[h:h + 1, :] * acc_s[h] + _dot(vblk[g * DV:(g + 1) * DV, :], p_s[h])

    def off_diagonal(j, carry):
        attend(j, False)
        return carry

    lax.fori_loop(0, i, off_diagonal, 0)
    attend(i, True)

    for h in range(N_ATT_HEADS):
        out_t = acc_s[h, 0:D, :] * (1.0 / acc_s[h, D:D + 1, :])
        o_ref[:, h * D:(h + 1) * D] = out_t.T


def _dsa_prompt(h2d, kb3, vt3, kib3, *, S, Qb, k_sel):
    nkb = S // Qb
    const3 = lambda shp: pl.BlockSpec(shp, lambda i: (0, 0, 0), pipeline_mode=pl.Buffered(1))
    return pl.pallas_call(
        functools.partial(_dsa_prompt_kernel, Qb=Qb, k_sel=k_sel),
        grid=(nkb,),
        in_specs=[
            pl.BlockSpec((Qb, D_ATT), lambda i: (i, OFF_Q // D_ATT)),
            pl.BlockSpec((Qb, IDX_HEADS * IDX_DIM), lambda i: (i, OFF_QI // (IDX_HEADS * IDX_DIM))),
            pl.BlockSpec((Qb, 128), lambda i: (i, OFF_SMALL // 128)),
            const3((N_ATT_HEADS, ATT_HEAD_DIM, 1)),
            const3((nkb, Qb, KV_DIM)),
            const3((nkb, N_KV_HEADS * V_AUG_ROWS, Qb)),
            const3((nkb, Qb, IDX_DIM)),
        ],
        out_specs=pl.BlockSpec((Qb, D_ATT), lambda i: (i, 0)),
        out_shape=jax.ShapeDtypeStruct((S, D_ATT), F32),
        scratch_shapes=[
            pltpu.VMEM((nkb + COUNT_UNROLL - 1, Qb, Qb), F32),
            pltpu.VMEM((N_ATT_HEADS, Qb, Qb), F32),
            pltpu.VMEM((N_ATT_HEADS, Qb, Qb), BF16),
            pltpu.VMEM((N_ATT_HEADS, 2 * ATT_HEAD_DIM, Qb), BF16),
            pltpu.VMEM((N_ATT_HEADS, Qb), F32),
            pltpu.VMEM((N_ATT_HEADS, Qb), F32),
            pltpu.VMEM((N_ATT_HEADS, V_AUG_ROWS, Qb), F32),
        ],
        compiler_params=pltpu.CompilerParams(
            dimension_semantics=("arbitrary",), vmem_limit_bytes=VMEM_LIMIT),
        name="dsa_prompt",
    )(h2d, h2d, h2d, jnp.asarray(_alibi_rows()), kb3, vt3, kib3)


def _dsa_sample_kernel(qr_ref, qir_ref, wr_ref, slope_ref, ck_ref, cv_ref, cki_ref, kn_ref, vn_ref, kin_ref,
                       o_ref, keys_s, sel_s, *, T, P, k_sel):
    LN = 128
    L = P + LN
    rpg = N_ATT_HEADS // N_KV_HEADS
    R = rpg * T

    qi_b = qir_ref[0].astype(BF16)
    w_col = wr_ref[0] * (IDX_DIM ** -0.5 * IDX_HEADS ** -0.5)

    def head_sum(s):
        return (jnp.maximum(s, 0.0) * w_col).reshape(IDX_HEADS, T, s.shape[-1]).sum(axis=0)

    sc_c = head_sum(_dot(qi_b, cki_ref[0].astype(BF16)))
    sc_n = head_sum(_dot(qi_b, kin_ref[0].astype(BF16), _NT))
    keys_s[:, 0:P] = sc_c
    col_n = lax.broadcasted_iota(I32, (T, LN), 1)
    keys_s[:, P:L] = jnp.where(col_n < T, sc_n, -jnp.inf)

    def count(pred):
        return jnp.sum(jnp.where(pred(keys_s[...]), 1, 0).astype(I32), axis=1, keepdims=True)

    def max_below(bound):
        sc = keys_s[...]
        return jnp.max(jnp.where(sc < bound, sc, -jnp.inf), axis=1, keepdims=True)

    sc_all = keys_s[...]
    thr, _, few = _kth_largest(
        count, max_below, k_sel,
        jnp.min(jnp.where(sc_all > -jnp.inf, sc_all, jnp.inf), axis=1, keepdims=True),
        jnp.max(sc_all, axis=1, keepdims=True), count(lambda s: s > -jnp.inf))
    need = jnp.where(few, jnp.float32(3e38), (k_sel - count(lambda s: s > thr)).astype(F32))

    triu = (lax.broadcasted_iota(I32, (LN, LN), 0) <= lax.broadcasted_iota(I32, (LN, LN), 1)).astype(BF16)
    seen = jnp.zeros((T, 1), F32)
    for jb in range(L // LN):
        kk = keys_s[:, jb * LN:(jb + 1) * LN]
        eq = kk == thr
        rank = seen + _dot(jnp.where(eq, 1.0, 0.0).astype(BF16), triu)
        sel_s[:, jb * LN:(jb + 1) * LN] = jnp.where((kk > thr) | (eq & (rank <= need)), 1.0, 0.0)
        seen = rank[:, LN - 1:LN]

    qpos = P + lax.broadcasted_iota(I32, (T, L), 0)
    kpos = lax.broadcasted_iota(I32, (T, L), 1)
    dist = jnp.abs(qpos - kpos).astype(F32)
    dist_r = jnp.concatenate([dist] * rpg, axis=0)
    sel_r = jnp.concatenate([sel_s[...]] * rpg, axis=0) > 0.5
    for g in range(N_KV_HEADS):
        sl = slice(g * ATT_HEAD_DIM, (g + 1) * ATT_HEAD_DIM)
        qg = (qr_ref[0, g * R:(g + 1) * R, :] * (ATT_HEAD_DIM ** -0.5)).astype(BF16)
        cache_rows = pl.ds(g, P, stride=N_KV_HEADS)
        lg = jnp.concatenate([_dot(qg, ck_ref[0, cache_rows, :].astype(BF16), _NT),
                              _dot(qg, kn_ref[0, :, sl].astype(BF16), _NT)], axis=1)
        lg = jnp.where(sel_r, lg - slope_ref[g * R:(g + 1) * R, :] * dist_r, -jnp.inf)
        m = jnp.max(lg, axis=1, keepdims=True)
        p = jnp.exp(lg - m)
        den = jnp.sum(p, axis=1, keepdims=True)
        pb = p.astype(BF16)
        out = (_dot(pb[:, 0:P], cv_ref[0, cache_rows, :].astype(BF16))
               + _dot(pb[:, P:L], vn_ref[0, :, sl].astype(BF16)))
        out = out * (1.0 / den)
        for r in range(rpg):
            hh = g * rpg + r
            o_ref[0, :, hh * ATT_HEAD_DIM:(hh + 1) * ATT_HEAD_DIM] = out[r * T:(r + 1) * T, :]


def _dsa_sample(q_rows, qi_rows, w_rows, slope_rows, ck, cv, cki, kn, vn, kin, *, Bd, T, P, k_sel):
    LN = 128
    per_b = lambda shp: pl.BlockSpec(shp, lambda b: (b, 0, 0))
    return pl.pallas_call(
        functools.partial(_dsa_sample_kernel, T=T, P=P, k_sel=k_sel),
        grid=(Bd,),
        in_specs=[
            per_b((1, N_ATT_HEADS * T, ATT_HEAD_DIM)),
            per_b((1, IDX_HEADS * T, IDX_DIM)),
            per_b((1, IDX_HEADS * T, 1)),
            pl.BlockSpec((N_ATT_HEADS * T, 1), lambda b: (0, 0)),
            per_b((1, N_KV_HEADS * P, ATT_HEAD_DIM)),
            per_b((1, N_KV_HEADS * P, ATT_HEAD_DIM)),
            per_b((1, IDX_DIM, P)),
            per_b((1, LN, KV_DIM)),
            per_b((1, LN, KV_DIM)),
            per_b((1, LN, IDX_DIM)),
        ],
        out_specs=per_b((1, T, D_ATT)),
        out_shape=jax.ShapeDtypeStruct((Bd, T, D_ATT), F32),
        scratch_shapes=[
            pltpu.VMEM((T, P + LN), F32),
            pltpu.VMEM((T, P + LN), F32),
        ],
        compiler_params=pltpu.CompilerParams(
            dimension_semantics=("arbitrary",), vmem_limit_bytes=VMEM_LIMIT),
        name="dsa_sample",
    )(q_rows, qi_rows, w_rows, slope_rows, ck, cv, cki, kn, vn, kin)


def _merge_kernel(x_ref, ys_ref, att_ref, za_ref, wo_ref, g_ref, b_ref, o_ref):
    att = att_ref[...] * _silu(za_ref[...])
    mix = _dot(ys_ref[...].astype(BF16), wo_ref[0:D_SSM, :]) + _dot(att.astype(BF16), wo_ref[D_SSM:, :])
    hres = ALPHA * x_ref[...] + mix
    mu = jnp.mean(hres, axis=-1, keepdims=True)
    hc = hres - mu
    var = jnp.mean(hc * hc, axis=-1, keepdims=True)
    o_ref[...] = hc * lax.rsqrt(var + LN_EPS) * g_ref[...] + b_ref[...]


def _merge(x2d, y_ssd, att, h2d, w_out_b, ln_g, ln_b, tm):
    m = x2d.shape[0]
    rows = lambda i: (i, 0)
    return pl.pallas_call(
        _merge_kernel,
        grid=(m // tm,),
        in_specs=[
            pl.BlockSpec((tm, D_MODEL), rows),
            pl.BlockSpec((tm, D_SSM), rows),
            pl.BlockSpec((tm, D_ATT), rows),
            pl.BlockSpec((tm, D_ATT), lambda i: (i, OFF_ZA // D_ATT)),
            pl.BlockSpec((D_SSM + D_ATT, D_MODEL), lambda i: (0, 0), pipeline_mode=pl.Buffered(1)),
            pl.BlockSpec((1, D_MODEL), lambda i: (0, 0)),
            pl.BlockSpec((1, D_MODEL), lambda i: (0, 0)),
        ],
        out_specs=pl.BlockSpec((tm, D_MODEL), rows),
        out_shape=jax.ShapeDtypeStruct((m, D_MODEL), F32),
        compiler_params=pltpu.CompilerParams(
            dimension_semantics=("arbitrary",), vmem_limit_bytes=VMEM_LIMIT),
        name="merge",
    )(x2d, y_ssd, att, h2d, w_out_b, ln_g, ln_b)


def _regroup_w_in(w_in):
    s = _SRC
    cols = [w_in[:, s["zs"]:s["zs"] + 1024], w_in[:, s["q"]:s["q"] + 1024], w_in[:, s["za"]:s["za"] + 1024],
            w_in[:, s["xbc"]:s["xbc"] + CONV_DIM], w_in[:, s["qi"]:s["qi"] + 512],
            w_in[:, s["k"]:s["k"] + KV_DIM], w_in[:, s["v"]:s["v"] + KV_DIM],
            w_in[:, s["dt"]:s["dt"] + 16], w_in[:, s["ki"]:s["ki"] + IDX_DIM], w_in[:, s["wi"]:s["wi"] + IDX_HEADS],
            jnp.zeros((D_MODEL, 128 - 16 - IDX_DIM - IDX_HEADS), w_in.dtype)]
    return jnp.concatenate(cols, axis=1).astype(BF16)


def _row_tile(m):
    return 256 if m % 256 == 0 else m


def kernel(x_prompt, x_sample, cache_k, cache_v, cache_kidx, state_ssm, state_conv, w_in, conv_w, conv_b,
           dt_bias, a_log, d_skip, ssd_norm_w, w_out, ln_g, ln_b):
    B, S, _ = x_prompt.shape
    Bd, T, _ = x_sample.shape
    P = cache_k.shape[1]
    assert B == 1 and S % 256 == 0 and T % 8 == 0 and T >= CONV_W - 1 and T <= 128 and P % 128 == 0
    HP = N_SSD_HEADS * SSD_HEAD_DIM

    w_pad = _regroup_w_in(w_in)
    w_out_b = w_out.astype(BF16)
    conv_b2 = conv_b.reshape(1, CONV_DIM)
    dtb2 = dt_bias.reshape(1, N_SSD_HEADS)
    alog2 = a_log.reshape(1, N_SSD_HEADS)
    dskip_full = jnp.repeat(d_skip, SSD_HEAD_DIM).reshape(1, D_SSM)
    nw2 = ssd_norm_w.reshape(1, D_SSM)
    g2 = ln_g.reshape(1, D_MODEL)
    b2 = ln_b.reshape(1, D_MODEL)

    xp = x_prompt.reshape(S, D_MODEL)
    Qb = _row_tile(S)
    h_p, k_p, v_p, ki_p, kb3, vt3, kib3 = _project(xp, w_pad, Qb)
    conv0 = jnp.zeros((1, 8, CONV_DIM), F32)
    ssm0 = jnp.zeros((1, HP, D_STATE), F32)
    yssd_p, ssm_p = _ssd(h_p, conv0, ssm0, conv_w, conv_b2, dtb2, alog2, dskip_full, nw2,
                         B=1, L=S, Q=CHUNK, Lb=256)
    att_p = _dsa_prompt(h_p, kb3, vt3, kib3, S=S, Qb=Qb, k_sel=min(TOPK_MAX, S // 4))
    y_p = _merge(xp, yssd_p, att_p, h_p, w_out_b, g2, b2, _row_tile(S))

    M = Bd * T
    xs = x_sample.reshape(M, D_MODEL)
    h_s, k_s, v_s, ki_s = _project(xs, w_pad, _row_tile(M))[:4]
    conv_prev8 = jnp.pad(state_conv, ((0, 0), (8 - (CONV_W - 1), 0), (0, 0)))
    yssd_s, ssm_s = _ssd(h_s, conv_prev8, state_ssm.reshape(Bd, HP, D_STATE), conv_w, conv_b2, dtb2, alog2,
                         dskip_full, nw2, B=Bd, L=T, Q=T, Lb=T)
    h_s3 = h_s.reshape(Bd, T, D_PAD)

    def head_rows(a, nh, dh):
        return a.reshape(Bd, T, nh, dh).transpose(0, 2, 1, 3).reshape(Bd, nh * T, dh)

    q_rows = head_rows(h_s3[:, :, OFF_Q:OFF_Q + D_ATT], N_ATT_HEADS, ATT_HEAD_DIM)
    qi_rows = head_rows(h_s3[:, :, OFF_QI:OFF_QI + IDX_HEADS * IDX_DIM], IDX_HEADS, IDX_DIM)
    w_rows = head_rows(h_s3[:, :, OFF_SMALL + SM_WI:OFF_SMALL + SM_WI + IDX_HEADS], IDX_HEADS, 1)
    slope_rows = jnp.asarray(np.repeat([_alibi_slope(h) for h in range(N_ATT_HEADS)], T).reshape(-1, 1), F32)
    padn = lambda a: jnp.pad(a.reshape(Bd, T, -1), ((0, 0), (0, 128 - T), (0, 0)))
    att_s = _dsa_sample(q_rows, qi_rows, w_rows, slope_rows,
                        cache_k.reshape(Bd, N_KV_HEADS * P, ATT_HEAD_DIM),
                        cache_v.reshape(Bd, N_KV_HEADS * P, ATT_HEAD_DIM), cache_kidx.transpose(0, 2, 1),
                        padn(k_s), padn(v_s), padn(ki_s), Bd=Bd, T=T, P=P, k_sel=min(TOPK_MAX, (P + T) // 4))
    y_s = _merge(xs, yssd_s, att_s.reshape(M, D_ATT), h_s, w_out_b, g2, b2, _row_tile(M))

    kv4 = lambda a, b_, l: a.reshape(b_, l, N_KV_HEADS, ATT_HEAD_DIM)
    st4 = lambda a, b_: a.reshape(b_, N_SSD_HEADS, SSD_HEAD_DIM, D_STATE)
    conv_p = h_p[S - (CONV_W - 1):, OFF_XBC:OFF_XBC + CONV_DIM].reshape(1, CONV_W - 1, CONV_DIM)
    conv_s = h_s3[:, T - (CONV_W - 1):, OFF_XBC:OFF_XBC + CONV_DIM]
    return (y_p.reshape(1, S, D_MODEL), y_s.reshape(Bd, T, D_MODEL),
            kv4(k_p, 1, S), kv4(v_p, 1, S), ki_p.reshape(1, S, IDX_DIM), st4(ssm_p, 1), conv_p,
            kv4(k_s, Bd, T), kv4(v_s, Bd, T), ki_s.reshape(Bd, T, IDX_DIM), st4(ssm_s, Bd), conv_s)
```

```python
import functools

import numpy as np
import jax
import jax.numpy as jnp
from jax import lax
from jax.experimental import pallas as pl
from jax.experimental.pallas import tpu as pltpu

F32 = jnp.float32
BF16 = jnp.bfloat16
I32 = jnp.int32

D_MODEL = 1024
CHUNK = 64
D_SSM = 1024
SSD_HEAD_DIM = 64
N_SSD_HEADS = 16
N_SSD_GROUPS = 2
D_STATE = 128
CONV_W = 4
CONV_DIM = D_SSM + 2 * N_SSD_GROUPS * D_STATE
D_ATT = 1024
ATT_HEAD_DIM = 128
N_ATT_HEADS = 8
N_KV_HEADS = 2
KV_DIM = N_KV_HEADS * ATT_HEAD_DIM
IDX_HEADS = 8
IDX_DIM = 64
TOPK_MAX = 256
ALPHA = 2.0 ** 0.25
LN_EPS = 1e-5
RMS_EPS = 1e-5

_SRC = dict(zs=0, xbc=1024, dt=2560, q=2576, k=3600, v=3856, za=4112, qi=5136, ki=5648, wi=5712)
OFF_ZS, OFF_Q, OFF_ZA, OFF_XBC, OFF_QI, OFF_K, OFF_V, OFF_SMALL = 0, 1024, 2048, 3072, 4608, 5120, 5376, 5632
D_PAD = 5760
SM_DT, SM_KI, SM_WI = 0, 16, 80

NEG_BIG = -1e30

VMEM_LIMIT = 56 * 1024 * 1024


def _dot(a, b, dims=(((1,), (0,)), ((), ())), precision=None):
    return lax.dot_general(a, b, dims, precision=precision, preferred_element_type=F32)


_NT = (((1,), (1,)), ((), ()))
_TN = (((0,), (0,)), ((), ()))


def _silu(x):
    return x * (1.0 / (1.0 + jnp.exp(-x)))


def _softplus(x):
    return jnp.maximum(x, 0.0) + jnp.log1p(jnp.exp(-jnp.abs(x)))


F32_LOWEST = float(np.finfo(np.float32).min)
BISECT_PASSES = 14


def _kth_largest(count, max_below, k_sel, s_min, s_max, n_valid):
    few = n_valid < k_sel
    lo = s_min
    hi = s_max + jnp.maximum(jnp.abs(s_max) * 2.0 ** -20, 1e-30)

    def halve(_, carry):
        lo, hi, n_lo = carry
        mid = 0.5 * lo + 0.5 * hi
        cnt = count(lambda s: s >= mid)
        ok = cnt >= k_sel
        return jnp.where(ok, mid, lo), jnp.where(ok, hi, mid), jnp.where(ok, cnt, n_lo)

    lo, hi, n_lo = lax.fori_loop(0, BISECT_PASSES, halve, (lo, hi, n_valid))
    exact_lo = jnp.logical_and(jnp.logical_not(few), n_lo == k_sel)

    def finish(carry):
        hi, thr, n_ge, todo, _ = carry
        v = max_below(hi)
        cnt = count(lambda s: s >= v)
        found = (todo > 0) & (cnt >= k_sel)
        thr = jnp.where(found, v, thr)
        n_ge = jnp.where(found, cnt, n_ge)
        todo = jnp.where(found, 0, todo)
        return jnp.where(todo > 0, v, hi), thr, n_ge, todo, jnp.max(todo)

    todo = jnp.where(jnp.logical_or(few, exact_lo), 0, 1).astype(I32)
    init = (hi, jnp.where(exact_lo, lo, jnp.float32(F32_LOWEST)), jnp.full(hi.shape, k_sel, I32), todo,
            jnp.max(todo))
    _, thr, n_ge, _, _ = lax.while_loop(lambda c: c[4] > 0, finish, init)
    return thr, n_ge, few


def _split_bf16x3(x):
    parts, r = [], np.float32(x)
    for _ in range(3):
        p = np.float32(np.asarray(r, dtype=BF16))
        parts.append(float(p))
        r = np.float32(r - p)
    assert r == 0.0
    return parts


_PROJ_CHUNKS = tuple((c, 512) for c in range(0, 5632, 512)) + ((5632, 128),)


def _proj_kernel(x_ref, w_ref, h_ref, k_ref, v_ref, ki_ref, kb_ref, vt_ref, kib_ref):
    xb = x_ref[...].astype(BF16)
    for c0, cw in _PROJ_CHUNKS:
        h_ref[:, c0:c0 + cw] = _dot(xb, w_ref[:, c0:c0 + cw])
    k = h_ref[:, OFF_K:OFF_K + KV_DIM]
    v = h_ref[:, OFF_V:OFF_V + KV_DIM]
    ki = h_ref[:, OFF_SMALL + SM_KI:OFF_SMALL + SM_KI + IDX_DIM]
    tm = k.shape[0]
    for g in range(N_KV_HEADS):
        k_ref[pl.ds(g, tm, stride=N_KV_HEADS), :] = k[:, g * ATT_HEAD_DIM:(g + 1) * ATT_HEAD_DIM]
        v_ref[pl.ds(g, tm, stride=N_KV_HEADS), :] = v[:, g * ATT_HEAD_DIM:(g + 1) * ATT_HEAD_DIM]
    ki_ref[...] = ki
    kb_ref[0] = k.astype(BF16)
    kib_ref[0] = ki.astype(BF16)
    v_t = v.T
    for g in range(N_KV_HEADS):
        r0 = g * V_AUG_ROWS
        vt_ref[0, r0:r0 + ATT_HEAD_DIM, :] = v_t[g * ATT_HEAD_DIM:(g + 1) * ATT_HEAD_DIM, :].astype(BF16)
        vt_ref[0, r0 + ATT_HEAD_DIM:r0 + V_AUG_ROWS, :] = jnp.ones(
            (V_AUG_ROWS - ATT_HEAD_DIM, v_t.shape[1]), BF16)


def _project(x2d, w_pad, tm):
    m = x2d.shape[0]
    assert m % tm == 0
    return pl.pallas_call(
        _proj_kernel,
        grid=(m // tm,),
        in_specs=[
            pl.BlockSpec((tm, D_MODEL), lambda i: (i, 0)),
            pl.BlockSpec((D_MODEL, D_PAD), lambda i: (0, 0), pipeline_mode=pl.Buffered(1)),
        ],
        out_specs=[
            pl.BlockSpec((tm, D_PAD), lambda i: (i, 0)),
            pl.BlockSpec((N_KV_HEADS * tm, ATT_HEAD_DIM), lambda i: (i, 0)),
            pl.BlockSpec((N_KV_HEADS * tm, ATT_HEAD_DIM), lambda i: (i, 0)),
            pl.BlockSpec((tm, IDX_DIM), lambda i: (i, 0)),
            pl.BlockSpec((1, tm, KV_DIM), lambda i: (i, 0, 0)),
            pl.BlockSpec((1, N_KV_HEADS * V_AUG_ROWS, tm), lambda i: (i, 0, 0)),
            pl.BlockSpec((1, tm, IDX_DIM), lambda i: (i, 0, 0)),
        ],
        out_shape=[
            jax.ShapeDtypeStruct((m, D_PAD), F32),
            jax.ShapeDtypeStruct((N_KV_HEADS * m, ATT_HEAD_DIM), F32),
            jax.ShapeDtypeStruct((N_KV_HEADS * m, ATT_HEAD_DIM), F32),
            jax.ShapeDtypeStruct((m, IDX_DIM), F32),
            jax.ShapeDtypeStruct((m // tm, tm, KV_DIM), BF16),
            jax.ShapeDtypeStruct((m // tm, N_KV_HEADS * V_AUG_ROWS, tm), BF16),
            jax.ShapeDtypeStruct((m // tm, tm, IDX_DIM), BF16),
        ],
        compiler_params=pltpu.CompilerParams(
            dimension_semantics=("arbitrary",), vmem_limit_bytes=VMEM_LIMIT),
        name="proj",
    )(x2d, w_pad)


def _ssd_kernel(zs_ref, xbc_ref, sm_ref, cprev_ref, sprev_ref, cw_ref, cb_ref, dtb_ref, alog_ref,
                dskip_ref, nw_ref, y_ref, snew_ref, xpad, xc_s, state, *, Q, Lb):
    i = pl.program_id(1)
    nblk = pl.num_programs(1)
    H, P, N = N_SSD_HEADS, SSD_HEAD_DIM, D_STATE
    HQ = H * Q
    hp = min(H, 256 // Q)
    n_diag = H // hp
    GP = (H // N_SSD_GROUPS) * P

    @pl.when(i == 0)
    def _():
        xpad[0:8, :] = cprev_ref[0]
        state[...] = sprev_ref[0].T

    @pl.when(i > 0)
    def _():
        xpad[0:8, :] = xpad[Lb:Lb + 8, :]

    xpad[8:8 + Lb, :] = xbc_ref[...]
    conv = cb_ref[...] + xpad[5:5 + Lb, :] * cw_ref[0:1, :]
    conv = conv + xpad[6:6 + Lb, :] * cw_ref[1:2, :]
    conv = conv + xpad[7:7 + Lb, :] * cw_ref[2:3, :]
    conv = conv + xpad[8:8 + Lb, :] * cw_ref[3:4, :]
    xc_s[...] = _silu(conv)

    a_neg = -jnp.exp(alog_ref[...])

    def iota(shape, d):
        return lax.broadcasted_iota(I32, shape, d)

    lq = Q.bit_length() - 1
    e_p = (iota((H, H * P), 1) >> 6 == iota((H, H * P), 0)).astype(BF16)
    e_q = (iota((H, HQ), 1) >> lq == iota((H, HQ), 0)).astype(BF16)
    j_of = iota((Q, HQ), 1) & (Q - 1)
    t_of = iota((Q, HQ), 0)
    u_rep = (t_of <= j_of).astype(F32)
    causal = j_of <= t_of
    c6 = iota((Q, 6 * Q), 1) & (2 * Q - 1)
    r6 = iota((Q, 6 * Q), 0)
    seg_lhs3 = jnp.where(c6 < Q, (c6 <= r6).astype(F32), -1.0).astype(BF16)
    tril3 = ((iota((Q, 3 * Q), 1) & (Q - 1)) <= iota((Q, 3 * Q), 0)).astype(BF16)
    bd_mask = (iota((hp * Q, hp * P), 0) >> lq) == (iota((hp * Q, hp * P), 1) >> 6)

    def split3(x):
        x1 = x.astype(BF16)
        r1 = x - x1.astype(F32)
        x2 = r1.astype(BF16)
        return x1, x2, (r1 - x2.astype(F32)).astype(BF16)

    def expand(x, e):
        rows = x.shape[0]
        r = _dot(jnp.concatenate(split3(x), axis=0), e)
        return (r[0:rows] + r[rows:2 * rows]) + r[2 * rows:3 * rows]

    def chunk(c, carry):
        r0 = pl.multiple_of(c * Q, Q)
        dt = _softplus(sm_ref[pl.ds(r0, Q), SM_DT:SM_DT + H] + dtb_ref[...])
        a = dt * a_neg
        both = expand(jnp.concatenate([dt, a], axis=0), e_p)
        dt_full, a_full = both[:Q], both[Q:]
        acum_full = _dot(tril3, jnp.concatenate(split3(a_full), axis=0))
        tot_full = acum_full[Q - 1:Q, :]
        xs = xc_s[pl.ds(r0, Q), 0:D_SSM]
        bm = xc_s[pl.ds(r0, Q), D_SSM:D_SSM + N_SSD_GROUPS * N].astype(BF16)
        cm = xc_s[pl.ds(r0, Q), D_SSM + N_SSD_GROUPS * N:CONV_DIM].astype(BF16)
        xd = xs * dt_full
        xdw = (xd * jnp.exp(tot_full - acum_full)).astype(BF16)
        xdb = xd.astype(BF16)

        a_q = expand(a, e_q)
        seg_rhs = jnp.concatenate([a_q, a_q * u_rep], axis=0)
        seg = _dot(seg_lhs3, jnp.concatenate(split3(seg_rhs), axis=0))
        decay = jnp.exp(jnp.where(causal, seg, -jnp.inf))
        cb_parts = []
        for g in range(N_SSD_GROUPS):
            bm_g = bm[:, g * N:(g + 1) * N]
            rep = jnp.concatenate([bm_g] * (H // N_SSD_GROUPS), axis=0)
            cb_parts.append(_dot(cm[:, g * N:(g + 1) * N], rep, _NT))
        mmat = (jnp.concatenate(cb_parts, axis=1) * decay).astype(BF16)

        y_parts = []
        for d in range(n_diag):
            xd_d = xdb[:, d * hp * P:(d + 1) * hp * P]
            bd = jnp.where(bd_mask, jnp.concatenate([xd_d] * hp, axis=0), jnp.zeros((), BF16))
            y_parts.append(_dot(mmat[:, d * hp * Q:(d + 1) * hp * Q], bd))
        y_diag = jnp.concatenate(y_parts, axis=1) if n_diag > 1 else y_parts[0]

        chunk_decay = jnp.exp(tot_full)
        y_off_parts = []
        for g in range(N_SSD_GROUPS):
            st_g = state[:, g * GP:(g + 1) * GP]
            y_off_parts.append(_dot(cm[:, g * N:(g + 1) * N], st_g.astype(BF16)))
            upd = _dot(bm[:, g * N:(g + 1) * N], xdw[:, g * GP:(g + 1) * GP], _TN)
            state[:, g * GP:(g + 1) * GP] = chunk_decay[:, g * GP:(g + 1) * GP] * st_g + upd
        y_off = jnp.concatenate(y_off_parts, axis=1) * jnp.exp(acum_full)

        y = (y_diag + y_off) + dskip_ref[...] * xs
        gt = y * _silu(zs_ref[pl.ds(r0, Q), :])
        for g in range(N_SSD_GROUPS):
            gg = gt[:, g * GP:(g + 1) * GP]
            ms = jnp.mean(gg * gg, axis=-1, keepdims=True)
            y_ref[pl.ds(r0, Q), g * GP:(g + 1) * GP] = (
                gg * lax.rsqrt(ms + RMS_EPS) * nw_ref[:, g * GP:(g + 1) * GP])
        return carry

    lax.fori_loop(0, Lb // Q, chunk, 0)

    @pl.when(i == nblk - 1)
    def _():
        snew_ref[0] = state[...].T


def _ssd(h2d, conv_prev8, ssm_prev, conv_w, conv_b, dt_bias, a_log, dskip_full, norm_w, *, B, L, Q, Lb):
    assert L % Lb == 0 and Lb % Q == 0 and Lb % 8 == 0
    nblk = L // Lb
    HP = N_SSD_HEADS * SSD_HEAD_DIM
    row = lambda b, i: b * nblk + i
    full2 = lambda shp: pl.BlockSpec(shp, lambda b, i: (0, 0))
    return pl.pallas_call(
        functools.partial(_ssd_kernel, Q=Q, Lb=Lb),
        grid=(B, nblk),
        in_specs=[
            pl.BlockSpec((Lb, D_SSM), lambda b, i: (row(b, i), OFF_ZS // D_SSM)),
            pl.BlockSpec((Lb, CONV_DIM), lambda b, i: (row(b, i), OFF_XBC // CONV_DIM)),
            pl.BlockSpec((Lb, 128), lambda b, i: (row(b, i), OFF_SMALL // 128)),
            pl.BlockSpec((1, 8, CONV_DIM), lambda b, i: (b, 0, 0)),
            pl.BlockSpec((1, HP, D_STATE), lambda b, i: (b, 0, 0)),
            full2((CONV_W, CONV_DIM)),
            full2((1, CONV_DIM)),
            full2((1, N_SSD_HEADS)),
            full2((1, N_SSD_HEADS)),
            full2((1, D_SSM)),
            full2((1, D_SSM)),
        ],
        out_specs=[
            pl.BlockSpec((Lb, D_SSM), lambda b, i: (row(b, i), 0)),
            pl.BlockSpec((1, HP, D_STATE), lambda b, i: (b, 0, 0)),
        ],
        out_shape=[
            jax.ShapeDtypeStruct((B * L, D_SSM), F32),
            jax.ShapeDtypeStruct((B, HP, D_STATE), F32),
        ],
        scratch_shapes=[
            pltpu.VMEM((Lb + 8, CONV_DIM), F32),
            pltpu.VMEM((Lb, CONV_DIM), F32),
            pltpu.VMEM((D_STATE, HP), F32),
        ],
        compiler_params=pltpu.CompilerParams(
            dimension_semantics=("arbitrary", "arbitrary"), vmem_limit_bytes=VMEM_LIMIT),
        name="ssd",
    )(h2d, h2d, h2d, conv_prev8, ssm_prev, conv_w, conv_b, dt_bias, a_log, dskip_full, norm_w)


def _alibi_slope(h):
    return float(2.0 ** (-8.0 * (h + 1) / N_ATT_HEADS))


LOG2E = float(np.log2(np.e))
POS_SPLIT = 128
N_POS_ROWS = 6
V_AUG_ROWS = ATT_HEAD_DIM + 16
COUNT_UNROLL = 4


def _alibi_rows():
    rows = np.zeros((N_ATT_HEADS, ATT_HEAD_DIM, 1), np.float32)
    for h in range(N_ATT_HEADS):
        pieces = _split_bf16x3(np.float32(_alibi_slope(h) * LOG2E))
        rows[h, 0:3, 0] = [p * POS_SPLIT for p in pieces]
        rows[h, 3:6, 0] = pieces
    return rows


def _dsa_prompt_kernel(q_ref, qi_ref, sm_ref, arow_ref, kb_ref, vt_ref, kib_ref, o_ref,
                       sc_s, lg_s, p_s, qa_s, m_s, a_s, acc_s, *, Qb, k_sel):
    Kb = Qb
    i = pl.program_id(0)
    nk = i + 1
    rpg = N_ATT_HEADS // N_KV_HEADS
    D = ATT_HEAD_DIM
    DV = V_AUG_ROWS

    q_t = q_ref[...].T * (D ** -0.5 * LOG2E)
    for h in range(N_ATT_HEADS):
        qa_s[h, 0:D, :] = q_t[h * D:(h + 1) * D, :].astype(BF16)
        qa_s[h, D:2 * D, :] = jnp.broadcast_to(arow_ref[h], (D, Qb)).astype(BF16)
    qi_t = qi_ref[...].T.astype(BF16)
    w_t = sm_ref[...].T[SM_WI:SM_WI + IDX_HEADS, :] * (IDX_DIM ** -0.5 * IDX_HEADS ** -0.5)
    qpos = i * Qb + lax.broadcasted_iota(I32, (1, Qb), 1)
    chunk_end = (((qpos >> 6) + 1) << 6) - 1
    krow = lax.broadcasted_iota(I32, (Kb, 1), 0)

    @pl.when(i == 0)
    def _():
        sc_s[...] = jnp.full(sc_s.shape, -jnp.inf, F32)

    def score_blk(j, carry):
        kib = kib_ref[j]
        score = jnp.zeros((Kb, Qb), F32)
        for h in range(IDX_HEADS):
            s = _dot(kib, qi_t[h * IDX_DIM:(h + 1) * IDX_DIM, :])
            score = score + w_t[h:h + 1, :] * jnp.maximum(s, 0.0)
        score = jnp.where((j * Kb + krow) <= chunk_end, score, -jnp.inf)
        sc_s[j] = score
        s_min, s_max, n_valid = carry
        valid = score > -jnp.inf
        fold = lambda a: a.reshape(Kb // 32, 32, Qb)
        s_min = jnp.minimum(s_min, fold(jnp.where(valid, score, jnp.inf)).min(axis=0))
        s_max = jnp.maximum(s_max, fold(score).max(axis=0))
        return s_min, s_max, n_valid + fold(jnp.where(valid, 1, 0).astype(I32)).sum(axis=0)

    s_min, s_max, n_valid = lax.fori_loop(
        0, nk, score_blk,
        (jnp.full((32, Qb), jnp.inf, F32), jnp.full((32, Qb), -jnp.inf, F32), jnp.zeros((32, Qb), I32)))
    s_min = s_min.min(axis=0, keepdims=True)
    s_max = s_max.max(axis=0, keepdims=True)
    n_valid = n_valid.sum(axis=0, keepdims=True)

    trips = lax.shift_right_logical(nk + (COUNT_UNROLL - 1), COUNT_UNROLL.bit_length() - 1)

    def count(pred):
        def body(jj, acc):
            for u in range(COUNT_UNROLL):
                hit = jnp.where(pred(sc_s[jj * COUNT_UNROLL + u]), 1, 0).astype(I32)
                acc = acc + hit.reshape(Kb // 32, 32, Qb).sum(axis=0)
            return acc
        return lax.fori_loop(0, trips, body, jnp.zeros((32, Qb), I32)).sum(axis=0, keepdims=True)

    def max_below(bound):
        def body(jj, acc):
            for u in range(COUNT_UNROLL):
                sc = sc_s[jj * COUNT_UNROLL + u]
                acc = jnp.maximum(acc, jnp.where(sc < bound, sc, -jnp.inf).reshape(Kb // 32, 32, Qb).max(axis=0))
            return acc
        return lax.fori_loop(0, trips, body, jnp.full((32, Qb), -jnp.inf, F32)).max(axis=0, keepdims=True)

    thr, n_ge, few = _kth_largest(count, max_below, k_sel, s_min, s_max, n_valid)
    tied = n_ge > k_sel
    NO_LIMIT = 2 ** 30

    def last_tie_positions():
        need = (k_sel - count(lambda s: s > thr)).astype(F32)
        tril = (lax.broadcasted_iota(I32, (Kb, Kb), 1) <= lax.broadcasted_iota(I32, (Kb, Kb), 0)).astype(BF16)

        def blk(j, carry):
            seen, pos = carry
            eq = jnp.where(sc_s[j] == thr, 1.0, 0.0)
            here = jnp.sum(eq, axis=0, keepdims=True)
            crossing = tied & (seen < need) & (seen + here >= need)

            def locate():
                rank = seen + _dot(tril, eq.astype(BF16))
                before = jnp.sum(jnp.where(rank < need, 1, 0).astype(I32), axis=0, keepdims=True)
                return j * Kb + before

            pos_j = lax.cond(jnp.max(jnp.where(crossing, 1, 0)) > 0, locate, lambda: jnp.zeros((1, Qb), I32))
            return seen + here, jnp.where(crossing, pos_j, pos)

        init = (jnp.zeros((1, Qb), F32), jnp.full((1, Qb), NO_LIMIT, I32))
        return lax.fori_loop(0, nk, blk, init)[1]

    tie_pos = lax.cond(jnp.max(jnp.where(tied, 1, 0)) > 0, last_tie_positions,
                       lambda: jnp.full((1, Qb), NO_LIMIT, I32))

    m_s[...] = jnp.full(m_s.shape, NEG_BIG, F32)
    acc_s[...] = jnp.zeros(acc_s.shape, F32)

    def attend(j, diagonal):
        sc = sc_s[j]
        kblk = kb_ref[j]
        kpos = j * Kb + krow
        bias = jnp.where(sc > thr, 0.0,
                         jnp.where(sc == thr, jnp.where(kpos <= tie_pos, 0.0, -jnp.inf), -jnp.inf))
        if diagonal:
            adj = jnp.minimum(kpos, 2 * qpos - kpos).astype(F32)
        else:
            lane = lax.broadcasted_iota(I32, (Kb, D), 1)
            hi = (kpos >> 7).astype(F32)
            lo = (kpos & (POS_SPLIT - 1)).astype(F32)
            feat = jnp.where(lane < 3, hi, jnp.where(lane < N_POS_ROWS, lo, 0.0)).astype(BF16)
            kcat = [jnp.concatenate([kblk[:, g * D:(g + 1) * D], feat], axis=1) for g in range(N_KV_HEADS)]
        for h in range(N_ATT_HEADS):
            g = h // rpg
            if diagonal:
                lg = _dot(kblk[:, g * D:(g + 1) * D], qa_s[h, 0:D, :]) + (_alibi_slope(h) * LOG2E) * adj
            else:
                lg = _dot(kcat[g], qa_s[h])
            lg = lg + bias
            lg_s[h] = lg
            m_old = m_s[h:h + 1, :]
            m_new = jnp.maximum(m_old, jnp.max(lg, axis=0, keepdims=True))
            a_s[h:h + 1, :] = jnp.exp2(m_old - m_new)
            m_s[h:h + 1, :] = m_new
        for h in range(N_ATT_HEADS):
            p_s[h] = jnp.exp2(lg_s[h] - m_s[h:h + 1, :]).astype(BF16)
        vblk = vt_ref[j]
        for h in range(N_ATT_HEADS):
            g = h // rpg
            acc_s[h] = a_s[h:h + 1, :] * acc_s[h] + _dot(vblk[g * DV:(g + 1) * DV, :], p_s[h])

    def off_diagonal(j, carry):
        attend(j, False)
        return carry

    lax.fori_loop(0, i, off_diagonal, 0)
    attend(i, True)

    for h in range(N_ATT_HEADS):
        out_t = acc_s[h, 0:D, :] * (1.0 / acc_s[h, D:D + 1, :])
        o_ref[:, h * D:(h + 1) * D] = out_t.T


def _dsa_prompt(h2d, kb3, vt3, kib3, *, S, Qb, k_sel):
    nkb = S // Qb
    const3 = lambda shp: pl.BlockSpec(shp, lambda i: (0, 0, 0), pipeline_mode=pl.Buffered(1))
    return pl.pallas_call(
        functools.partial(_dsa_prompt_kernel, Qb=Qb, k_sel=k_sel),
        grid=(nkb,),
        in_specs=[
            pl.BlockSpec((Qb, D_ATT), lambda i: (i, OFF_Q // D_ATT)),
            pl.BlockSpec((Qb, IDX_HEADS * IDX_DIM), lambda i: (i, OFF_QI // (IDX_HEADS * IDX_DIM))),
            pl.BlockSpec((Qb, 128), lambda i: (i, OFF_SMALL // 128)),
            const3((N_ATT_HEADS, ATT_HEAD_DIM, 1)),
            const3((nkb, Qb, KV_DIM)),
            const3((nkb, N_KV_HEADS * V_AUG_ROWS, Qb)),
            const3((nkb, Qb, IDX_DIM)),
        ],
        out_specs=pl.BlockSpec((Qb, D_ATT), lambda i: (i, 0)),
        out_shape=jax.ShapeDtypeStruct((S, D_ATT), F32),
        scratch_shapes=[
            pltpu.VMEM((nkb + COUNT_UNROLL - 1, Qb, Qb), F32),
            pltpu.VMEM((N_ATT_HEADS, Qb, Qb), F32),
            pltpu.VMEM((N_ATT_HEADS, Qb, Qb), BF16),
            pltpu.VMEM((N_ATT_HEADS, 2 * ATT_HEAD_DIM, Qb), BF16),
            pltpu.VMEM((N_ATT_HEADS, Qb), F32),
            pltpu.VMEM((N_ATT_HEADS, Qb), F32),
            pltpu.VMEM((N_ATT_HEADS, V_AUG_ROWS, Qb), F32),
        ],
        compiler_params=pltpu.CompilerParams(
            dimension_semantics=("arbitrary",), vmem_limit_bytes=VMEM_LIMIT),
        name="dsa_prompt",
    )(h2d, h2d, h2d, jnp.asarray(_alibi_rows()), kb3, vt3, kib3)


def _dsa_sample_kernel(qr_ref, qir_ref, wr_ref, slope_ref, ck_ref, cv_ref, cki_ref, kn_ref, vn_ref, kin_ref,
                       o_ref, keys_s, sel_s, *, T, P, k_sel):
    LN = 128
    L = P + LN
    rpg = N_ATT_HEADS // N_KV_HEADS
    R = rpg * T

    qi_b = qir_ref[0].astype(BF16)
    w_col = wr_ref[0] * (IDX_DIM ** -0.5 * IDX_HEADS ** -0.5)

    def head_sum(s):
        return (jnp.maximum(s, 0.0) * w_col).reshape(IDX_HEADS, T, s.shape[-1]).sum(axis=0)

    sc_c = head_sum(_dot(qi_b, cki_ref[0].astype(BF16)))
    sc_n = head_sum(_dot(qi_b, kin_ref[0].astype(BF16), _NT))
    keys_s[:, 0:P] = sc_c
    col_n = lax.broadcasted_iota(I32, (T, LN), 1)
    keys_s[:, P:L] = jnp.where(col_n < T, sc_n, -jnp.inf)

    def count(pred):
        return jnp.sum(jnp.where(pred(keys_s[...]), 1, 0).astype(I32), axis=1, keepdims=True)

    def max_below(bound):
        sc = keys_s[...]
        return jnp.max(jnp.where(sc < bound, sc, -jnp.inf), axis=1, keepdims=True)

    sc_all = keys_s[...]
    thr, _, few = _kth_largest(
        count, max_below, k_sel,
        jnp.min(jnp.where(sc_all > -jnp.inf, sc_all, jnp.inf), axis=1, keepdims=True),
        jnp.max(sc_all, axis=1, keepdims=True), count(lambda s: s > -jnp.inf))
    need = jnp.where(few, jnp.float32(3e38), (k_sel - count(lambda s: s > thr)).astype(F32))

    triu = (lax.broadcasted_iota(I32, (LN, LN), 0) <= lax.broadcasted_iota(I32, (LN, LN), 1)).astype(BF16)
    seen = jnp.zeros((T, 1), F32)
    for jb in range(L // LN):
        kk = keys_s[:, jb * LN:(jb + 1) * LN]
        eq = kk == thr
        rank = seen + _dot(jnp.where(eq, 1.0, 0.0).astype(BF16), triu)
        sel_s[:, jb * LN:(jb + 1) * LN] = jnp.where((kk > thr) | (eq & (rank <= need)), 1.0, 0.0)
        seen = rank[:, LN - 1:LN]

    qpos = P + lax.broadcasted_iota(I32, (T, L), 0)
    kpos = lax.broadcasted_iota(I32, (T, L), 1)
    dist = jnp.abs(qpos - kpos).astype(F32)
    dist_r = jnp.concatenate([dist] * rpg, axis=0)
    sel_r = jnp.concatenate([sel_s[...]] * rpg, axis=0) > 0.5
    for g in range(N_KV_HEADS):
        sl = slice(g * ATT_HEAD_DIM, (g + 1) * ATT_HEAD_DIM)
        qg = (qr_ref[0, g * R:(g + 1) * R, :] * (ATT_HEAD_DIM ** -0.5)).astype(BF16)
        cache_rows = pl.ds(g, P, stride=N_KV_HEADS)
        lg = jnp.concatenate([_dot(qg, ck_ref[0, cache_rows, :].astype(BF16), _NT),
                              _dot(qg, kn_ref[0, :, sl].astype(BF16), _NT)], axis=1)
        lg = jnp.where(sel_r, lg - slope_ref[g * R:(g + 1) * R, :] * dist_r, -jnp.inf)
        m = jnp.max(lg, axis=1, keepdims=True)
        p = jnp.exp(lg - m)
        den = jnp.sum(p, axis=1, keepdims=True)
        pb = p.astype(BF16)
        out = (_dot(pb[:, 0:P], cv_ref[0, cache_rows, :].astype(BF16))
               + _dot(pb[:, P:L], vn_ref[0, :, sl].astype(BF16)))
        out = out * (1.0 / den)
        for r in range(rpg):
            hh = g * rpg + r
            o_ref[0, :, hh * ATT_HEAD_DIM:(hh + 1) * ATT_HEAD_DIM] = out[r * T:(r + 1) * T, :]


def _dsa_sample(q_rows, qi_rows, w_rows, slope_rows, ck, cv, cki, kn, vn, kin, *, Bd, T, P, k_sel):
    LN = 128
    per_b = lambda shp: pl.BlockSpec(shp, lambda b: (b, 0, 0))
    return pl.pallas_call(
        functools.partial(_dsa_sample_kernel, T=T, P=P, k_sel=k_sel),
        grid=(Bd,),
        in_specs=[
            per_b((1, N_ATT_HEADS * T, ATT_HEAD_DIM)),
            per_b((1, IDX_HEADS * T, IDX_DIM)),
            per_b((1, IDX_HEADS * T, 1)),
            pl.BlockSpec((N_ATT_HEADS * T, 1), lambda b: (0, 0)),
            per_b((1, N_KV_HEADS * P, ATT_HEAD_DIM)),
            per_b((1, N_KV_HEADS * P, ATT_HEAD_DIM)),
            per_b((1, IDX_DIM, P)),
            per_b((1, LN, KV_DIM)),
            per_b((1, LN, KV_DIM)),
            per_b((1, LN, IDX_DIM)),
        ],
        out_specs=per_b((1, T, D_ATT)),
        out_shape=jax.ShapeDtypeStruct((Bd, T, D_ATT), F32),
        scratch_shapes=[
            pltpu.VMEM((T, P + LN), F32),
            pltpu.VMEM((T, P + LN), F32),
        ],
        compiler_params=pltpu.CompilerParams(
            dimension_semantics=("arbitrary",), vmem_limit_bytes=VMEM_LIMIT),
        name="dsa_sample",
    )(q_rows, qi_rows, w_rows, slope_rows, ck, cv, cki, kn, vn, kin)


def _merge_kernel(x_ref, ys_ref, att_ref, za_ref, wo_ref, g_ref, b_ref, o_ref):
    att = att_ref[...] * _silu(za_ref[...])
    mix = _dot(ys_ref[...].astype(BF16), wo_ref[0:D_SSM, :]) + _dot(att.astype(BF16), wo_ref[D_SSM:, :])
    hres = ALPHA * x_ref[...] + mix
    mu = jnp.mean(hres, axis=-1, keepdims=True)
    hc = hres - mu
    var = jnp.mean(hc * hc, axis=-1, keepdims=True)
    o_ref[...] = hc * lax.rsqrt(var + LN_EPS) * g_ref[...] + b_ref[...]


def _merge(x2d, y_ssd, att, h2d, w_out_b, ln_g, ln_b, tm):
    m = x2d.shape[0]
    rows = lambda i: (i, 0)
    return pl.pallas_call(
        _merge_kernel,
        grid=(m // tm,),
        in_specs=[
            pl.BlockSpec((tm, D_MODEL), rows),
            pl.BlockSpec((tm, D_SSM), rows),
            pl.BlockSpec((tm, D_ATT), rows),
            pl.BlockSpec((tm, D_ATT), lambda i: (i, OFF_ZA // D_ATT)),
            pl.BlockSpec((D_SSM + D_ATT, D_MODEL), lambda i: (0, 0), pipeline_mode=pl.Buffered(1)),
            pl.BlockSpec((1, D_MODEL), lambda i: (0, 0)),
            pl.BlockSpec((1, D_MODEL), lambda i: (0, 0)),
        ],
        out_specs=pl.BlockSpec((tm, D_MODEL), rows),
        out_shape=jax.ShapeDtypeStruct((m, D_MODEL), F32),
        compiler_params=pltpu.CompilerParams(
            dimension_semantics=("arbitrary",), vmem_limit_bytes=VMEM_LIMIT),
        name="merge",
    )(x2d, y_ssd, att, h2d, w_out_b, ln_g, ln_b)


def _regroup_w_in(w_in):
    s = _SRC
    cols = [w_in[:, s["zs"]:s["zs"] + 1024], w_in[:, s["q"]:s["q"] + 1024], w_in[:, s["za"]:s["za"] + 1024],
            w_in[:, s["xbc"]:s["xbc"] + CONV_DIM], w_in[:, s["qi"]:s["qi"] + 512],
            w_in[:, s["k"]:s["k"] + KV_DIM], w_in[:, s["v"]:s["v"] + KV_DIM],
            w_in[:, s["dt"]:s["dt"] + 16], w_in[:, s["ki"]:s["ki"] + IDX_DIM], w_in[:, s["wi"]:s["wi"] + IDX_HEADS],
            jnp.zeros((D_MODEL, 128 - 16 - IDX_DIM - IDX_HEADS), w_in.dtype)]
    return jnp.concatenate(cols, axis=1).astype(BF16)


def _row_tile(m):
    return 256 if m % 256 == 0 else m


def kernel(x_prompt, x_sample, cache_k, cache_v, cache_kidx, state_ssm, state_conv, w_in, conv_w, conv_b,
           dt_bias, a_log, d_skip, ssd_norm_w, w_out, ln_g, ln_b):
    B, S, _ = x_prompt.shape
    Bd, T, _ = x_sample.shape
    P = cache_k.shape[1]
    assert B == 1 and S % 256 == 0 and T % 8 == 0 and T >= CONV_W - 1 and T <= 128 and P % 128 == 0
    HP = N_SSD_HEADS * SSD_HEAD_DIM

    w_pad = _regroup_w_in(w_in)
    w_out_b = w_out.astype(BF16)
    conv_b2 = conv_b.reshape(1, CONV_DIM)
    dtb2 = dt_bias.reshape(1, N_SSD_HEADS)
    alog2 = a_log.reshape(1, N_SSD_HEADS)
    dskip_full = jnp.repeat(d_skip, SSD_HEAD_DIM).reshape(1, D_SSM)
    nw2 = ssd_norm_w.reshape(1, D_SSM)
    g2 = ln_g.reshape(1, D_MODEL)
    b2 = ln_b.reshape(1, D_MODEL)

    xp = x_prompt.reshape(S, D_MODEL)
    Qb = _row_tile(S)
    h_p, k_p, v_p, ki_p, kb3, vt3, kib3 = _project(xp, w_pad, Qb)
    conv0 = jnp.zeros((1, 8, CONV_DIM), F32)
    ssm0 = jnp.zeros((1, HP, D_STATE), F32)
    yssd_p, ssm_p = _ssd(h_p, conv0, ssm0, conv_w, conv_b2, dtb2, alog2, dskip_full, nw2,
                         B=1, L=S, Q=CHUNK, Lb=256)
    att_p = _dsa_prompt(h_p, kb3, vt3, kib3, S=S, Qb=Qb, k_sel=min(TOPK_MAX, S // 4))
    y_p = _merge(xp, yssd_p, att_p, h_p, w_out_b, g2, b2, _row_tile(S))

    M = Bd * T
    xs = x_sample.reshape(M, D_MODEL)
    h_s, k_s, v_s, ki_s = _project(xs, w_pad, _row_tile(M))[:4]
    conv_prev8 = jnp.pad(state_conv, ((0, 0), (8 - (CONV_W - 1), 0), (0, 0)))
    yssd_s, ssm_s = _ssd(h_s, conv_prev8, state_ssm.reshape(Bd, HP, D_STATE), conv_w, conv_b2, dtb2, alog2,
                         dskip_full, nw2, B=Bd, L=T, Q=T, Lb=T)
    h_s3 = h_s.reshape(Bd, T, D_PAD)

    def head_rows(a, nh, dh):
        return a.reshape(Bd, T, nh, dh).transpose(0, 2, 1, 3).reshape(Bd, nh * T, dh)

    q_rows = head_rows(h_s3[:, :, OFF_Q:OFF_Q + D_ATT], N_ATT_HEADS, ATT_HEAD_DIM)
    qi_rows = head_rows(h_s3[:, :, OFF_QI:OFF_QI + IDX_HEADS * IDX_DIM], IDX_HEADS, IDX_DIM)
    w_rows = head_rows(h_s3[:, :, OFF_SMALL + SM_WI:OFF_SMALL + SM_WI + IDX_HEADS], IDX_HEADS, 1)
    slope_rows = jnp.asarray(np.repeat([_alibi_slope(h) for h in range(N_ATT_HEADS)], T).reshape(-1, 1), F32)
    padn = lambda a: jnp.pad(a.reshape(Bd, T, -1), ((0, 0), (0, 128 - T), (0, 0)))
    att_s = _dsa_sample(q_rows, qi_rows, w_rows, slope_rows,
                        cache_k.reshape(Bd, N_KV_HEADS * P, ATT_HEAD_DIM),
                        cache_v.reshape(Bd, N_KV_HEADS * P, ATT_HEAD_DIM), cache_kidx.transpose(0, 2, 1),
                        padn(k_s), padn(v_s), padn(ki_s), Bd=Bd, T=T, P=P, k_sel=min(TOPK_MAX, (P + T) // 4))
    y_s = _merge(xs, yssd_s, att_s.reshape(M, D_ATT), h_s, w_out_b, g2, b2, _row_tile(M))

    kv4 = lambda a, b_, l: a.reshape(b_, l, N_KV_HEADS, ATT_HEAD_DIM)
    st4 = lambda a, b_: a.reshape(b_, N_SSD_HEADS, SSD_HEAD_DIM, D_STATE)
    conv_p = h_p[S - (CONV_W - 1):, OFF_XBC:OFF_XBC + CONV_DIM].reshape(1, CONV_W - 1, CONV_DIM)
    conv_s = h_s3[:, T - (CONV_W - 1):, OFF_XBC:OFF_XBC + CONV_DIM]
    return (y_p.reshape(1, S, D_MODEL), y_s.reshape(Bd, T, D_MODEL),
            kv4(k_p, 1, S), kv4(v_p, 1, S), ki_p.reshape(1, S, IDX_DIM), st4(ssm_p, 1), conv_p,
            kv4(k_s, Bd, T), kv4(v_s, Bd, T), ki_s.reshape(Bd, T, IDX_DIM), st4(ssm_s, Bd), conv_s)
```

```python
import functools

import numpy as np
import jax
import jax.numpy as jnp
from jax import lax
from jax.experimental import pallas as pl
from jax.experimental.pallas import tpu as pltpu

F32 = jnp.float32
BF16 = jnp.bfloat16
I32 = jnp.int32

D_MODEL = 1024
CHUNK = 64
D_SSM = 1024
SSD_HEAD_DIM = 64
N_SSD_HEADS = 16
N_SSD_GROUPS = 2
D_STATE = 128
CONV_W = 4
CONV_DIM = D_SSM + 2 * N_SSD_GROUPS * D_STATE
D_ATT = 1024
ATT_HEAD_DIM = 128
N_ATT_HEADS = 8
N_KV_HEADS = 2
KV_DIM = N_KV_HEADS * ATT_HEAD_DIM
IDX_HEADS = 8
IDX_DIM = 64
TOPK_MAX = 256
ALPHA = 2.0 ** 0.25
LN_EPS = 1e-5
RMS_EPS = 1e-5

_SRC = dict(zs=0, xbc=1024, dt=2560, q=2576, k=3600, v=3856, za=4112, qi=5136, ki=5648, wi=5712)
OFF_ZS, OFF_Q, OFF_ZA, OFF_XBC, OFF_QI, OFF_K, OFF_V, OFF_SMALL = 0, 1024, 2048, 3072, 4608, 5120, 5376, 5632
D_PAD = 5760
SM_DT, SM_KI, SM_WI = 0, 16, 80

NEG_BIG = -1e30

VMEM_LIMIT = 56 * 1024 * 1024


def _dot(a, b, dims=(((1,), (0,)), ((), ())), precision=None):
    return lax.dot_general(a, b, dims, precision=precision, preferred_element_type=F32)


_NT = (((1,), (1,)), ((), ()))
_TN = (((0,), (0,)), ((), ()))


def _silu(x):
    return x * (1.0 / (1.0 + jnp.exp(-x)))


def _softplus(x):
    return jnp.maximum(x, 0.0) + jnp.log1p(jnp.exp(-jnp.abs(x)))


F32_LOWEST = float(np.finfo(np.float32).min)
BISECT_PASSES = 14


def _kth_largest(count, max_below, k_sel, s_min, s_max, n_valid):
    few = n_valid < k_sel
    lo = s_min
    hi = s_max + jnp.maximum(jnp.abs(s_max) * 2.0 ** -20, 1e-30)

    def halve(_, carry):
        lo, hi, n_lo = carry
        mid = 0.5 * lo + 0.5 * hi
        cnt = count(lambda s: s >= mid)
        ok = cnt >= k_sel
        return jnp.where(ok, mid, lo), jnp.where(ok, hi, mid), jnp.where(ok, cnt, n_lo)

    lo, hi, n_lo = lax.fori_loop(0, BISECT_PASSES, halve, (lo, hi, n_valid))
    exact_lo = jnp.logical_and(jnp.logical_not(few), n_lo == k_sel)

    def finish(carry):
        hi, thr, n_ge, todo, _ = carry
        v = max_below(hi)
        cnt = count(lambda s: s >= v)
        found = (todo > 0) & (cnt >= k_sel)
        thr = jnp.where(found, v, thr)
        n_ge = jnp.where(found, cnt, n_ge)
        todo = jnp.where(found, 0, todo)
        return jnp.where(todo > 0, v, hi), thr, n_ge, todo, jnp.max(todo)

    todo = jnp.where(jnp.logical_or(few, exact_lo), 0, 1).astype(I32)
    init = (hi, jnp.where(exact_lo, lo, jnp.float32(F32_LOWEST)), jnp.full(hi.shape, k_sel, I32), todo,
            jnp.max(todo))
    _, thr, n_ge, _, _ = lax.while_loop(lambda c: c[4] > 0, finish, init)
    return thr, n_ge, few


def _split_bf16x3(x):
    parts, r = [], np.float32(x)
    for _ in range(3):
        p = np.float32(np.asarray(r, dtype=BF16))
        parts.append(float(p))
        r = np.float32(r - p)
    assert r == 0.0
    return parts


_PROJ_CHUNKS = tuple((c, 512) for c in range(0, 5632, 512)) + ((5632, 128),)


def _proj_kernel(x_ref, w_ref, h_ref, k_ref, v_ref, ki_ref, kb_ref, vt_ref, kib_ref):
    xb = x_ref[...].astype(BF16)
    for c0, cw in _PROJ_CHUNKS:
        h_ref[:, c0:c0 + cw] = _dot(xb, w_ref[:, c0:c0 + cw])
    k = h_ref[:, OFF_K:OFF_K + KV_DIM]
    v = h_ref[:, OFF_V:OFF_V + KV_DIM]
    ki = h_ref[:, OFF_SMALL + SM_KI:OFF_SMALL + SM_KI + IDX_DIM]
    tm = k.shape[0]
    for g in range(N_KV_HEADS):
        k_ref[pl.ds(g, tm, stride=N_KV_HEADS), :] = k[:, g * ATT_HEAD_DIM:(g + 1) * ATT_HEAD_DIM]
        v_ref[pl.ds(g, tm, stride=N_KV_HEADS), :] = v[:, g * ATT_HEAD_DIM:(g + 1) * ATT_HEAD_DIM]
    ki_ref[...] = ki
    kb_ref[0] = k.astype(BF16)
    kib_ref[0] = ki.astype(BF16)
    v_t = v.T
    for g in range(N_KV_HEADS):
        r0 = g * V_AUG_ROWS
        vt_ref[0, r0:r0 + ATT_HEAD_DIM, :] = v_t[g * ATT_HEAD_DIM:(g + 1) * ATT_HEAD_DIM, :].astype(BF16)
        vt_ref[0, r0 + ATT_HEAD_DIM:r0 + V_AUG_ROWS, :] = jnp.ones(
            (V_AUG_ROWS - ATT_HEAD_DIM, v_t.shape[1]), BF16)


def _project(x2d, w_pad, tm):
    m = x2d.shape[0]
    assert m % tm == 0
    return pl.pallas_call(
        _proj_kernel,
        grid=(m // tm,),
        in_specs=[
            pl.BlockSpec((tm, D_MODEL), lambda i: (i, 0)),
            pl.BlockSpec((D_MODEL, D_PAD), lambda i: (0, 0), pipeline_mode=pl.Buffered(1)),
        ],
        out_specs=[
            pl.BlockSpec((tm, D_PAD), lambda i: (i, 0)),
            pl.BlockSpec((N_KV_HEADS * tm, ATT_HEAD_DIM), lambda i: (i, 0)),
            pl.BlockSpec((N_KV_HEADS * tm, ATT_HEAD_DIM), lambda i: (i, 0)),
            pl.BlockSpec((tm, IDX_DIM), lambda i: (i, 0)),
            pl.BlockSpec((1, tm, KV_DIM), lambda i: (i, 0, 0)),
            pl.BlockSpec((1, N_KV_HEADS * V_AUG_ROWS, tm), lambda i: (i, 0, 0)),
            pl.BlockSpec((1, tm, IDX_DIM), lambda i: (i, 0, 0)),
        ],
        out_shape=[
            jax.ShapeDtypeStruct((m, D_PAD), F32),
            jax.ShapeDtypeStruct((N_KV_HEADS * m, ATT_HEAD_DIM), F32),
            jax.ShapeDtypeStruct((N_KV_HEADS * m, ATT_HEAD_DIM), F32),
            jax.ShapeDtypeStruct((m, IDX_DIM), F32),
            jax.ShapeDtypeStruct((m // tm, tm, KV_DIM), BF16),
            jax.ShapeDtypeStruct((m // tm, N_KV_HEADS * V_AUG_ROWS, tm), BF16),
            jax.ShapeDtypeStruct((m // tm, tm, IDX_DIM), BF16),
        ],
        compiler_params=pltpu.CompilerParams(
            dimension_semantics=("arbitrary",), vmem_limit_bytes=VMEM_LIMIT),
        name="proj",
    )(x2d, w_pad)


def _ssd_kernel(zs_ref, xbc_ref, sm_ref, cprev_ref, sprev_ref, cw_ref, cb_ref, dtb_ref, alog_ref,
                dskip_ref, nw_ref, y_ref, snew_ref, xpad, xc_s, state, *, Q, Lb):
    i = pl.program_id(1)
    nblk = pl.num_programs(1)
    H, P, N = N_SSD_HEADS, SSD_HEAD_DIM, D_STATE
    HQ = H * Q
    hp = min(H, 256 // Q)
    n_diag = H // hp
    GP = (H // N_SSD_GROUPS) * P

    @pl.when(i == 0)
    def _():
        xpad[0:8, :] = cprev_ref[0]
        state[...] = sprev_ref[0].T

    @pl.when(i > 0)
    def _():
        xpad[0:8, :] = xpad[Lb:Lb + 8, :]

    xpad[8:8 + Lb, :] = xbc_ref[...]
    conv = cb_ref[...] + xpad[5:5 + Lb, :] * cw_ref[0:1, :]
    conv = conv + xpad[6:6 + Lb, :] * cw_ref[1:2, :]
    conv = conv + xpad[7:7 + Lb, :] * cw_ref[2:3, :]
    conv = conv + xpad[8:8 + Lb, :] * cw_ref[3:4, :]
    xc_s[...] = _silu(conv)

    a_neg = -jnp.exp(alog_ref[...])

    def iota(shape, d):
        return lax.broadcasted_iota(I32, shape, d)

    lq = Q.bit_length() - 1
    lp = P.bit_length() - 1
    e_p = (iota((H, H * P), 1) >> lp == iota((H, H * P), 0)).astype(BF16)
    e_q = (iota((H, HQ), 1) >> lq == iota((H, HQ), 0)).astype(BF16)
    j_of = iota((Q, HQ), 1) & (Q - 1)
    t_of = iota((Q, HQ), 0)
    u_rep = (t_of <= j_of).astype(F32)
    causal = j_of <= t_of
    c6 = iota((Q, 6 * Q), 1) & (2 * Q - 1)
    r6 = iota((Q, 6 * Q), 0)
    seg_lhs3 = jnp.where(c6 < Q, (c6 <= r6).astype(F32), -1.0).astype(BF16)
    tril3 = ((iota((Q, 3 * Q), 1) & (Q - 1)) <= iota((Q, 3 * Q), 0)).astype(BF16)
    bd_mask = (iota((hp * Q, hp * P), 0) >> lq) == (iota((hp * Q, hp * P), 1) >> lp)

    def split3(x):
        x1 = x.astype(BF16)
        r1 = x - x1.astype(F32)
        x2 = r1.astype(BF16)
        return x1, x2, (r1 - x2.astype(F32)).astype(BF16)

    def expand(x, e):
        rows = x.shape[0]
        r = _dot(jnp.concatenate(split3(x), axis=0), e)
        return (r[0:rows] + r[rows:2 * rows]) + r[2 * rows:3 * rows]

    def chunk(c, carry):
        r0 = pl.multiple_of(c * Q, Q)
        dt = _softplus(sm_ref[pl.ds(r0, Q), SM_DT:SM_DT + H] + dtb_ref[...])
        a = dt * a_neg
        both = expand(jnp.concatenate([dt, a], axis=0), e_p)
        dt_full, a_full = both[:Q], both[Q:]
        acum_full = _dot(tril3, jnp.concatenate(split3(a_full), axis=0))
        tot_full = acum_full[Q - 1:Q, :]
        xs = xc_s[pl.ds(r0, Q), 0:D_SSM]
        bm = xc_s[pl.ds(r0, Q), D_SSM:D_SSM + N_SSD_GROUPS * N].astype(BF16)
        cm = xc_s[pl.ds(r0, Q), D_SSM + N_SSD_GROUPS * N:CONV_DIM].astype(BF16)
        xd = xs * dt_full
        xdw = (xd * jnp.exp(tot_full - acum_full)).astype(BF16)
        xdb = xd.astype(BF16)

        a_q = expand(a, e_q)
        seg_rhs = jnp.concatenate([a_q, a_q * u_rep], axis=0)
        seg = _dot(seg_lhs3, jnp.concatenate(split3(seg_rhs), axis=0))
        decay = jnp.exp(jnp.where(causal, seg, -jnp.inf))
        cb_parts = []
        for g in range(N_SSD_GROUPS):
            bm_g = bm[:, g * N:(g + 1) * N]
            rep = jnp.concatenate([bm_g] * (H // N_SSD_GROUPS), axis=0)
            cb_parts.append(_dot(cm[:, g * N:(g + 1) * N], rep, _NT))
        mmat = (jnp.concatenate(cb_parts, axis=1) * decay).astype(BF16)

        y_parts = []
        for d in range(n_diag):
            xd_d = xdb[:, d * hp * P:(d + 1) * hp * P]
            bd = jnp.where(bd_mask, jnp.concatenate([xd_d] * hp, axis=0), jnp.zeros((), BF16))
            y_parts.append(_dot(mmat[:, d * hp * Q:(d + 1) * hp * Q], bd))
        y_diag = jnp.concatenate(y_parts, axis=1) if n_diag > 1 else y_parts[0]

        chunk_decay = jnp.exp(tot_full)
        y_off_parts = []
        for g in range(N_SSD_GROUPS):
            st_g = state[:, g * GP:(g + 1) * GP]
            y_off_parts.append(_dot(cm[:, g * N:(g + 1) * N], st_g.astype(BF16)))
            upd = _dot(bm[:, g * N:(g + 1) * N], xdw[:, g * GP:(g + 1) * GP], _TN)
            state[:, g * GP:(g + 1) * GP] = chunk_decay[:, g * GP:(g + 1) * GP] * st_g + upd
        y_off = jnp.concatenate(y_off_parts, axis=1) * jnp.exp(acum_full)

        y = (y_diag + y_off) + dskip_ref[...] * xs
        gt = y * _silu(zs_ref[pl.ds(r0, Q), :])
        for g in range(N_SSD_GROUPS):
            gg = gt[:, g * GP:(g + 1) * GP]
            ms = jnp.mean(gg * gg, axis=-1, keepdims=True)
            y_ref[pl.ds(r0, Q), g * GP:(g + 1) * GP] = (
                gg * lax.rsqrt(ms + RMS_EPS) * nw_ref[:, g * GP:(g + 1) * GP])
        return carry

    lax.fori_loop(0, Lb // Q, chunk, 0)

    @pl.when(i == nblk - 1)
    def _():
        snew_ref[0] = state[...].T


def _ssd(h2d, conv_prev8, ssm_prev, conv_w, conv_b, dt_bias, a_log, dskip_full, norm_w, *, B, L, Q, Lb):
    assert L % Lb == 0 and Lb % Q == 0 and Lb % 8 == 0
    nblk = L // Lb
    HP = N_SSD_HEADS * SSD_HEAD_DIM
    row = lambda b, i: b * nblk + i
    full2 = lambda shp: pl.BlockSpec(shp, lambda b, i: (0, 0))
    return pl.pallas_call(
        functools.partial(_ssd_kernel, Q=Q, Lb=Lb),
        grid=(B, nblk),
        in_specs=[
            pl.BlockSpec((Lb, D_SSM), lambda b, i: (row(b, i), OFF_ZS // D_SSM)),
            pl.BlockSpec((Lb, CONV_DIM), lambda b, i: (row(b, i), OFF_XBC // CONV_DIM)),
            pl.BlockSpec((Lb, 128), lambda b, i: (row(b, i), OFF_SMALL // 128)),
            pl.BlockSpec((1, 8, CONV_DIM), lambda b, i: (b, 0, 0)),
            pl.BlockSpec((1, HP, D_STATE), lambda b, i: (b, 0, 0)),
            full2((CONV_W, CONV_DIM)),
            full2((1, CONV_DIM)),
            full2((1, N_SSD_HEADS)),
            full2((1, N_SSD_HEADS)),
            full2((1, D_SSM)),
            full2((1, D_SSM)),
        ],
        out_specs=[
            pl.BlockSpec((Lb, D_SSM), lambda b, i: (row(b, i), 0)),
            pl.BlockSpec((1, HP, D_STATE), lambda b, i: (b, 0, 0)),
        ],
        out_shape=[
            jax.ShapeDtypeStruct((B * L, D_SSM), F32),
            jax.ShapeDtypeStruct((B, HP, D_STATE), F32),
        ],
        scratch_shapes=[
            pltpu.VMEM((Lb + 8, CONV_DIM), F32),
            pltpu.VMEM((Lb, CONV_DIM), F32),
            pltpu.VMEM((D_STATE, HP), F32),
        ],
        compiler_params=pltpu.CompilerParams(
            dimension_semantics=("arbitrary", "arbitrary"), vmem_limit_bytes=VMEM_LIMIT),
        name="ssd",
    )(h2d, h2d, h2d, conv_prev8, ssm_prev, conv_w, conv_b, dt_bias, a_log, dskip_full, norm_w)


def _alibi_slope(h):
    return float(2.0 ** (-8.0 * (h + 1) / N_ATT_HEADS))


LOG2E = float(np.log2(np.e))
POS_SPLIT = 128
N_POS_ROWS = 6
V_AUG_ROWS = ATT_HEAD_DIM + 16
COUNT_UNROLL = 4


def _alibi_rows():
    rows = np.zeros((N_ATT_HEADS, ATT_HEAD_DIM, 1), np.float32)
    for h in range(N_ATT_HEADS):
        pieces = _split_bf16x3(np.float32(_alibi_slope(h) * LOG2E))
        rows[h, 0:3, 0] = [p * POS_SPLIT for p in pieces]
        rows[h, 3:6, 0] = pieces
    return rows


def _dsa_prompt_kernel(q_ref, qi_ref, sm_ref, arow_ref, kb_ref, vt_ref, kib_ref, o_ref,
                       sc_s, lg_s, p_s, qa_s, m_s, a_s, acc_s, *, Qb, k_sel):
    Kb = Qb
    i = pl.program_id(0)
    nk = i + 1
    rpg = N_ATT_HEADS // N_KV_HEADS
    D = ATT_HEAD_DIM
    DV = V_AUG_ROWS

    q_t = q_ref[...].T * (D ** -0.5 * LOG2E)
    for h in range(N_ATT_HEADS):
        qa_s[h, 0:D, :] = q_t[h * D:(h + 1) * D, :].astype(BF16)
        qa_s[h, D:2 * D, :] = jnp.broadcast_to(arow_ref[h], (D, Qb)).astype(BF16)
    qi_t = qi_ref[...].T.astype(BF16)
    w_t = sm_ref[...].T[SM_WI:SM_WI + IDX_HEADS, :] * (IDX_DIM ** -0.5 * IDX_HEADS ** -0.5)
    qpos = i * Qb + lax.broadcasted_iota(I32, (1, Qb), 1)
    chunk_end = (qpos | (CHUNK - 1))
    krow = lax.broadcasted_iota(I32, (Kb, 1), 0)

    @pl.when(i == 0)
    def _():
        sc_s[...] = jnp.full(sc_s.shape, -jnp.inf, F32)

    def score_blk(j, carry):
        kib = kib_ref[j]
        score = jnp.zeros((Kb, Qb), F32)
        for h in range(IDX_HEADS):
            s = _dot(kib, qi_t[h * IDX_DIM:(h + 1) * IDX_DIM, :])
            score = score + w_t[h:h + 1, :] * jnp.maximum(s, 0.0)
        score = jnp.where((j * Kb + krow) <= chunk_end, score, -jnp.inf)
        sc_s[j] = score
        s_min, s_max, n_valid = carry
        valid = score > -jnp.inf
        fold = lambda a: a.reshape(Kb // 32, 32, Qb)
        s_min = jnp.minimum(s_min, fold(jnp.where(valid, score, jnp.inf)).min(axis=0))
        s_max = jnp.maximum(s_max, fold(score).max(axis=0))
        return s_min, s_max, n_valid + fold(jnp.where(valid, 1, 0).astype(I32)).sum(axis=0)

    s_min, s_max, n_valid = lax.fori_loop(
        0, nk, score_blk,
        (jnp.full((32, Qb), jnp.inf, F32), jnp.full((32, Qb), -jnp.inf, F32), jnp.zeros((32, Qb), I32)))
    s_min = s_min.min(axis=0, keepdims=True)
    s_max = s_max.max(axis=0, keepdims=True)
    n_valid = n_valid.sum(axis=0, keepdims=True)

    trips = lax.shift_right_logical(nk + (COUNT_UNROLL - 1), COUNT_UNROLL.bit_length() - 1)

    def count(pred):
        def body(jj, acc):
            for u in range(COUNT_UNROLL):
                hit = jnp.where(pred(sc_s[jj * COUNT_UNROLL + u]), 1, 0).astype(I32)
                acc = acc + hit.reshape(Kb // 32, 32, Qb).sum(axis=0)
            return acc
        return lax.fori_loop(0, trips, body, jnp.zeros((32, Qb), I32)).sum(axis=0, keepdims=True)

    def max_below(bound):
        def body(jj, acc):
            for u in range(COUNT_UNROLL):
                sc = sc_s[jj * COUNT_UNROLL + u]
                acc = jnp.maximum(acc, jnp.where(sc < bound, sc, -jnp.inf).reshape(Kb // 32, 32, Qb).max(axis=0))
            return acc
        return lax.fori_loop(0, trips, body, jnp.full((32, Qb), -jnp.inf, F32)).max(axis=0, keepdims=True)

    thr, n_ge, few = _kth_largest(count, max_below, k_sel, s_min, s_max, n_valid)
    tied = n_ge > k_sel
    NO_LIMIT = 2 ** 30

    def last_tie_positions():
        need = (k_sel - count(lambda s: s > thr)).astype(F32)

        def scan(j, carry):
            seen, blk_of, before = carry
            here = jnp.sum(jnp.where(sc_s[j] == thr, 1.0, 0.0), axis=0, keepdims=True)
            crossing = tied & (seen < need) & (seen + here >= need)
            return seen + here, jnp.where(crossing, j, blk_of), jnp.where(crossing, seen, before)

        zero = jnp.zeros((1, Qb), F32)
        _, blk_of, before = lax.fori_loop(0, nk, scan, (zero, jnp.full((1, Qb), -1, I32), zero))
        tril = (lax.broadcasted_iota(I32, (Kb, Kb), 1) <= lax.broadcasted_iota(I32, (Kb, Kb), 0)).astype(BF16)

        def locate(carry):
            pos, blk_left, j = carry
            mine = blk_left == j
            eq = jnp.where(sc_s[j] == thr, 1.0, 0.0).astype(BF16)
            rank = before + _dot(tril, eq)
            rows = jnp.sum(jnp.where(rank < need, 1, 0).astype(I32), axis=0, keepdims=True)
            blk_left = jnp.where(mine, -1, blk_left)
            return jnp.where(mine, j * Kb + rows, pos), blk_left, jnp.max(blk_left)

        init = (jnp.full((1, Qb), NO_LIMIT, I32), blk_of, jnp.max(blk_of))
        return lax.while_loop(lambda c: c[2] >= 0, locate, init)[0]

    tie_pos = lax.cond(jnp.max(jnp.where(tied, 1, 0)) > 0, last_tie_positions,
                       lambda: jnp.full((1, Qb), NO_LIMIT, I32))

    m_s[...] = jnp.full(m_s.shape, NEG_BIG, F32)
    acc_s[...] = jnp.zeros(acc_s.shape, F32)

    def attend(j, diagonal):
        sc = sc_s[j]
        kblk = kb_ref[j]
        kpos = j * Kb + krow
        bias = jnp.where(sc > thr, 0.0,
                         jnp.where(sc == thr, jnp.where(kpos <= tie_pos, 0.0, -jnp.inf), -jnp.inf))
        if diagonal:
            adj = jnp.minimum(kpos, 2 * qpos - kpos).astype(F32)
        else:
            lane = lax.broadcasted_iota(I32, (Kb, D), 1)
            hi = (kpos >> (POS_SPLIT.bit_length() - 1)).astype(F32)
            lo = (kpos & (POS_SPLIT - 1)).astype(F32)
            feat = jnp.where(lane < 3, hi, jnp.where(lane < N_POS_ROWS, lo, 0.0)).astype(BF16)
            kcat = [jnp.concatenate([kblk[:, g * D:(g + 1) * D], feat], axis=1) for g in range(N_KV_HEADS)]
        for h in range(N_ATT_HEADS):
            g = h // rpg
            if diagonal:
                lg = _dot(kblk[:, g * D:(g + 1) * D], qa_s[h, 0:D, :]) + (_alibi_slope(h) * LOG2E) * adj
            else:
                lg = _dot(kcat[g], qa_s[h])
            lg = lg + bias
            lg_s[h] = lg
            m_old = m_s[h:h + 1, :]
            m_new = jnp.maximum(m_old, jnp.max(lg, axis=0, keepdims=True))
            a_s[h:h + 1, :] = jnp.exp2(m_old - m_new)
            m_s[h:h + 1, :] = m_new
        for h in range(N_ATT_HEADS):
            p_s[h] = jnp.exp2(lg_s[h] - m_s[h:h + 1, :]).astype(BF16)
        vblk = vt_ref[j]
        for h in range(N_ATT_HEADS):
            g = h // rpg
            acc_s[h] = a_s[h:h + 1, :] * acc_s[h] + _dot(vblk[g * DV:(g + 1) * DV, :], p_s[h])

    def off_diagonal(j, carry):
        attend(j, False)
        return carry

    lax.fori_loop(0, i, off_diagonal, 0)
    attend(i, True)

    for h in range(N_ATT_HEADS):
        out_t = acc_s[h, 0:D, :] * (1.0 / acc_s[h, D:D + 1, :])
        o_ref[:, h * D:(h + 1) * D] = out_t.T


def _dsa_prompt(h2d, kb3, vt3, kib3, *, S, Qb, k_sel):
    nkb = S // Qb
    const3 = lambda shp: pl.BlockSpec(shp, lambda i: (0, 0, 0), pipeline_mode=pl.Buffered(1))
    return pl.pallas_call(
        functools.partial(_dsa_prompt_kernel, Qb=Qb, k_sel=k_sel),
        grid=(nkb,),
        in_specs=[
            pl.BlockSpec((Qb, D_ATT), lambda i: (i, OFF_Q // D_ATT)),
            pl.BlockSpec((Qb, IDX_HEADS * IDX_DIM), lambda i: (i, OFF_QI // (IDX_HEADS * IDX_DIM))),
            pl.BlockSpec((Qb, 128), lambda i: (i, OFF_SMALL // 128)),
            const3((N_ATT_HEADS, ATT_HEAD_DIM, 1)),
            const3((nkb, Qb, KV_DIM)),
            const3((nkb, N_KV_HEADS * V_AUG_ROWS, Qb)),
            const3((nkb, Qb, IDX_DIM)),
        ],
        out_specs=pl.BlockSpec((Qb, D_ATT), lambda i: (i, 0)),
        out_shape=jax.ShapeDtypeStruct((S, D_ATT), F32),
        scratch_shapes=[
            pltpu.VMEM((nkb + COUNT_UNROLL - 1, Qb, Qb), F32),
            pltpu.VMEM((N_ATT_HEADS, Qb, Qb), F32),
            pltpu.VMEM((N_ATT_HEADS, Qb, Qb), BF16),
            pltpu.VMEM((N_ATT_HEADS, 2 * ATT_HEAD_DIM, Qb), BF16),
            pltpu.VMEM((N_ATT_HEADS, Qb), F32),
            pltpu.VMEM((N_ATT_HEADS, Qb), F32),
            pltpu.VMEM((N_ATT_HEADS, V_AUG_ROWS, Qb), F32),
        ],
        compiler_params=pltpu.CompilerParams(
            dimension_semantics=("arbitrary",), vmem_limit_bytes=VMEM_LIMIT),
        name="dsa_prompt",
    )(h2d, h2d, h2d, jnp.asarray(_alibi_rows()), kb3, vt3, kib3)


def _dsa_sample_kernel(qr_ref, qir_ref, wr_ref, slope_ref, ck_ref, cv_ref, cki_ref, kn_ref, vn_ref, kin_ref,
                       o_ref, keys_s, sel_s, *, T, P, k_sel):
    LN = 128
    L = P + LN
    rpg = N_ATT_HEADS // N_KV_HEADS
    R = rpg * T

    qi_b = qir_ref[0].astype(BF16)
    w_col = wr_ref[0] * (IDX_DIM ** -0.5 * IDX_HEADS ** -0.5)

    def head_sum(s):
        return (jnp.maximum(s, 0.0) * w_col).reshape(IDX_HEADS, T, s.shape[-1]).sum(axis=0)

    sc_c = head_sum(_dot(qi_b, cki_ref[0].astype(BF16)))
    sc_n = head_sum(_dot(qi_b, kin_ref[0].astype(BF16), _NT))
    keys_s[:, 0:P] = sc_c
    col_n = lax.broadcasted_iota(I32, (T, LN), 1)
    keys_s[:, P:L] = jnp.where(col_n < T, sc_n, -jnp.inf)

    def count(pred):
        return jnp.sum(jnp.where(pred(keys_s[...]), 1, 0).astype(I32), axis=1, keepdims=True)

    def max_below(bound):
        sc = keys_s[...]
        return jnp.max(jnp.where(sc < bound, sc, -jnp.inf), axis=1, keepdims=True)

    sc_all = keys_s[...]
    thr, _, few = _kth_largest(
        count, max_below, k_sel,
        jnp.min(jnp.where(sc_all > -jnp.inf, sc_all, jnp.inf), axis=1, keepdims=True),
        jnp.max(sc_all, axis=1, keepdims=True), count(lambda s: s > -jnp.inf))
    need = jnp.where(few, jnp.float32(3e38), (k_sel - count(lambda s: s > thr)).astype(F32))

    triu = (lax.broadcasted_iota(I32, (LN, LN), 0) <= lax.broadcasted_iota(I32, (LN, LN), 1)).astype(BF16)
    seen = jnp.zeros((T, 1), F32)
    for jb in range(L // LN):
        kk = keys_s[:, jb * LN:(jb + 1) * LN]
        eq = kk == thr
        rank = seen + _dot(jnp.where(eq, 1.0, 0.0).astype(BF16), triu)
        sel_s[:, jb * LN:(jb + 1) * LN] = jnp.where((kk > thr) | (eq & (rank <= need)), 1.0, 0.0)
        seen = rank[:, LN - 1:LN]

    qpos = P + lax.broadcasted_iota(I32, (T, L), 0)
    kpos = lax.broadcasted_iota(I32, (T, L), 1)
    dist = jnp.abs(qpos - kpos).astype(F32)
    dist_r = jnp.concatenate([dist] * rpg, axis=0)
    sel_r = jnp.concatenate([sel_s[...]] * rpg, axis=0) > 0.5
    for g in range(N_KV_HEADS):
        sl = slice(g * ATT_HEAD_DIM, (g + 1) * ATT_HEAD_DIM)
        qg = (qr_ref[0, g * R:(g + 1) * R, :] * (ATT_HEAD_DIM ** -0.5)).astype(BF16)
        cache_rows = pl.ds(g, P, stride=N_KV_HEADS)
        lg = jnp.concatenate([_dot(qg, ck_ref[0, cache_rows, :].astype(BF16), _NT),
                              _dot(qg, kn_ref[0, :, sl].astype(BF16), _NT)], axis=1)
        lg = jnp.where(sel_r, lg - slope_ref[g * R:(g + 1) * R, :] * dist_r, -jnp.inf)
        m = jnp.max(lg, axis=1, keepdims=True)
        p = jnp.exp(lg - m)
        den = jnp.sum(p, axis=1, keepdims=True)
        pb = p.astype(BF16)
        out = (_dot(pb[:, 0:P], cv_ref[0, cache_rows, :].astype(BF16))
               + _dot(pb[:, P:L], vn_ref[0, :, sl].astype(BF16)))
        out = out * (1.0 / den)
        for r in range(rpg):
            hh = g * rpg + r
            o_ref[0, :, hh * ATT_HEAD_DIM:(hh + 1) * ATT_HEAD_DIM] = out[r * T:(r + 1) * T, :]


def _dsa_sample(q_rows, qi_rows, w_rows, slope_rows, ck, cv, cki, kn, vn, kin, *, Bd, T, P, k_sel):
    LN = 128
    per_b = lambda shp: pl.BlockSpec(shp, lambda b: (b, 0, 0))
    return pl.pallas_call(
        functools.partial(_dsa_sample_kernel, T=T, P=P, k_sel=k_sel),
        grid=(Bd,),
        in_specs=[
            per_b((1, N_ATT_HEADS * T, ATT_HEAD_DIM)),
            per_b((1, IDX_HEADS * T, IDX_DIM)),
            per_b((1, IDX_HEADS * T, 1)),
            pl.BlockSpec((N_ATT_HEADS * T, 1), lambda b: (0, 0)),
            per_b((1, N_KV_HEADS * P, ATT_HEAD_DIM)),
            per_b((1, N_KV_HEADS * P, ATT_HEAD_DIM)),
            per_b((1, IDX_DIM, P)),
            per_b((1, LN, KV_DIM)),
            per_b((1, LN, KV_DIM)),
            per_b((1, LN, IDX_DIM)),
        ],
        out_specs=per_b((1, T, D_ATT)),
        out_shape=jax.ShapeDtypeStruct((Bd, T, D_ATT), F32),
        scratch_shapes=[
            pltpu.VMEM((T, P + LN), F32),
            pltpu.VMEM((T, P + LN), F32),
        ],
        compiler_params=pltpu.CompilerParams(
            dimension_semantics=("arbitrary",), vmem_limit_bytes=VMEM_LIMIT),
        name="dsa_sample",
    )(q_rows, qi_rows, w_rows, slope_rows, ck, cv, cki, kn, vn, kin)


def _merge_kernel(x_ref, ys_ref, att_ref, za_ref, wo_ref, g_ref, b_ref, o_ref):
    att = att_ref[...] * _silu(za_ref[...])
    mix = _dot(ys_ref[...].astype(BF16), wo_ref[0:D_SSM, :]) + _dot(att.astype(BF16), wo_ref[D_SSM:, :])
    hres = ALPHA * x_ref[...] + mix
    mu = jnp.mean(hres, axis=-1, keepdims=True)
    hc = hres - mu
    var = jnp.mean(hc * hc, axis=-1, keepdims=True)
    o_ref[...] = hc * lax.rsqrt(var + LN_EPS) * g_ref[...] + b_ref[...]


def _merge(x2d, y_ssd, att, h2d, w_out_b, ln_g, ln_b, tm):
    m = x2d.shape[0]
    rows = lambda i: (i, 0)
    return pl.pallas_call(
        _merge_kernel,
        grid=(m // tm,),
        in_specs=[
            pl.BlockSpec((tm, D_MODEL), rows),
            pl.BlockSpec((tm, D_SSM), rows),
            pl.BlockSpec((tm, D_ATT), rows),
            pl.BlockSpec((tm, D_ATT), lambda i: (i, OFF_ZA // D_ATT)),
            pl.BlockSpec((D_SSM + D_ATT, D_MODEL), lambda i: (0, 0), pipeline_mode=pl.Buffered(1)),
            pl.BlockSpec((1, D_MODEL), lambda i: (0, 0)),
            pl.BlockSpec((1, D_MODEL), lambda i: (0, 0)),
        ],
        out_specs=pl.BlockSpec((tm, D_MODEL), rows),
        out_shape=jax.ShapeDtypeStruct((m, D_MODEL), F32),
        compiler_params=pltpu.CompilerParams(
            dimension_semantics=("arbitrary",), vmem_limit_bytes=VMEM_LIMIT),
        name="merge",
    )(x2d, y_ssd, att, h2d, w_out_b, ln_g, ln_b)


def _regroup_w_in(w_in):
    s = _SRC
    cols = [w_in[:, s["zs"]:s["zs"] + 1024], w_in[:, s["q"]:s["q"] + 1024], w_in[:, s["za"]:s["za"] + 1024],
            w_in[:, s["xbc"]:s["xbc"] + CONV_DIM], w_in[:, s["qi"]:s["qi"] + 512],
            w_in[:, s["k"]:s["k"] + KV_DIM], w_in[:, s["v"]:s["v"] + KV_DIM],
            w_in[:, s["dt"]:s["dt"] + 16], w_in[:, s["ki"]:s["ki"] + IDX_DIM], w_in[:, s["wi"]:s["wi"] + IDX_HEADS],
            jnp.zeros((D_MODEL, 128 - 16 - IDX_DIM - IDX_HEADS), w_in.dtype)]
    return jnp.concatenate(cols, axis=1).astype(BF16)


def _row_tile(m):
    return 256 if m % 256 == 0 else m


def kernel(x_prompt, x_sample, cache_k, cache_v, cache_kidx, state_ssm, state_conv, w_in, conv_w, conv_b,
           dt_bias, a_log, d_skip, ssd_norm_w, w_out, ln_g, ln_b):
    B, S, _ = x_prompt.shape
    Bd, T, _ = x_sample.shape
    P = cache_k.shape[1]
    assert B == 1 and S % 256 == 0 and T % 8 == 0 and T >= CONV_W - 1 and T <= 128 and P % 128 == 0
    HP = N_SSD_HEADS * SSD_HEAD_DIM

    w_pad = _regroup_w_in(w_in)
    w_out_b = w_out.astype(BF16)
    conv_b2 = conv_b.reshape(1, CONV_DIM)
    dtb2 = dt_bias.reshape(1, N_SSD_HEADS)
    alog2 = a_log.reshape(1, N_SSD_HEADS)
    dskip_full = jnp.repeat(d_skip, SSD_HEAD_DIM).reshape(1, D_SSM)
    nw2 = ssd_norm_w.reshape(1, D_SSM)
    g2 = ln_g.reshape(1, D_MODEL)
    b2 = ln_b.reshape(1, D_MODEL)

    xp = x_prompt.reshape(S, D_MODEL)
    Qb = _row_tile(S)
    h_p, k_p, v_p, ki_p, kb3, vt3, kib3 = _project(xp, w_pad, Qb)
    conv0 = jnp.zeros((1, 8, CONV_DIM), F32)
    ssm0 = jnp.zeros((1, HP, D_STATE), F32)
    yssd_p, ssm_p = _ssd(h_p, conv0, ssm0, conv_w, conv_b2, dtb2, alog2, dskip_full, nw2,
                         B=1, L=S, Q=CHUNK, Lb=256)
    att_p = _dsa_prompt(h_p, kb3, vt3, kib3, S=S, Qb=Qb, k_sel=min(TOPK_MAX, S // 4))
    y_p = _merge(xp, yssd_p, att_p, h_p, w_out_b, g2, b2, _row_tile(S))

    M = Bd * T
    xs = x_sample.reshape(M, D_MODEL)
    h_s, k_s, v_s, ki_s = _project(xs, w_pad, _row_tile(M))[:4]
    conv_prev8 = jnp.pad(state_conv, ((0, 0), (8 - (CONV_W - 1), 0), (0, 0)))
    yssd_s, ssm_s = _ssd(h_s, conv_prev8, state_ssm.reshape(Bd, HP, D_STATE), conv_w, conv_b2, dtb2, alog2,
                         dskip_full, nw2, B=Bd, L=T, Q=T, Lb=T)
    h_s3 = h_s.reshape(Bd, T, D_PAD)

    def head_rows(a, nh, dh):
        return a.reshape(Bd, T, nh, dh).transpose(0, 2, 1, 3).reshape(Bd, nh * T, dh)

    q_rows = head_rows(h_s3[:, :, OFF_Q:OFF_Q + D_ATT], N_ATT_HEADS, ATT_HEAD_DIM)
    qi_rows = head_rows(h_s3[:, :, OFF_QI:OFF_QI + IDX_HEADS * IDX_DIM], IDX_HEADS, IDX_DIM)
    w_rows = head_rows(h_s3[:, :, OFF_SMALL + SM_WI:OFF_SMALL + SM_WI + IDX_HEADS], IDX_HEADS, 1)
    slope_rows = jnp.asarray(np.repeat([_alibi_slope(h) for h in range(N_ATT_HEADS)], T).reshape(-1, 1), F32)
    padn = lambda a: jnp.pad(a.reshape(Bd, T, -1), ((0, 0), (0, 128 - T), (0, 0)))
    att_s = _dsa_sample(q_rows, qi_rows, w_rows, slope_rows,
                        cache_k.reshape(Bd, N_KV_HEADS * P, ATT_HEAD_DIM),
                        cache_v.reshape(Bd, N_KV_HEADS * P, ATT_HEAD_DIM), cache_kidx.transpose(0, 2, 1),
                        padn(k_s), padn(v_s), padn(ki_s), Bd=Bd, T=T, P=P, k_sel=min(TOPK_MAX, (P + T) // 4))
    y_s = _merge(xs, yssd_s, att_s.reshape(M, D_ATT), h_s, w_out_b, g2, b2, _row_tile(M))

    kv4 = lambda a, b_, l: a.reshape(b_, l, N_KV_HEADS, ATT_HEAD_DIM)
    st4 = lambda a, b_: a.reshape(b_, N_SSD_HEADS, SSD_HEAD_DIM, D_STATE)
    conv_p = h_p[S - (CONV_W - 1):, OFF_XBC:OFF_XBC + CONV_DIM].reshape(1, CONV_W - 1, CONV_DIM)
    conv_s = h_s3[:, T - (CONV_W - 1):, OFF_XBC:OFF_XBC + CONV_DIM]
    return (y_p.reshape(1, S, D_MODEL), y_s.reshape(Bd, T, D_MODEL),
            kv4(k_p, 1, S), kv4(v_p, 1, S), ki_p.reshape(1, S, IDX_DIM), st4(ssm_p, 1), conv_p,
            kv4(k_s, Bd, T), kv4(v_s, Bd, T), ki_s.reshape(Bd, T, IDX_DIM), st4(ssm_s, Bd), conv_s)
```
